```python
import math
import jax, jax.numpy as jnp
from jax import lax
import numpy as np


D_MODEL = 1024
BATCH = 8
SEQ = 2048
DEPTH = 4
DEC_BATCH = 32
DEC_SEQ = 8
PAST_LEN = 8192
PAGE_SIZE = 128

MIX_WIDTH = D_MODEL
ATTN_WIDTH = MIX_WIDTH // 2
SSM_WIDTH = MIX_WIDTH - ATTN_WIDTH
HEAD_DIM = 64
N_HEADS = ATTN_WIDTH // HEAD_DIM
BRANCHES = ((128, 1), (512, 4), (2048, 16))
WIN_MAX = max(w for w, _ in BRANCHES)
SSM_GROUP_CH = 16
N_SSM_GROUPS = SSM_WIDTH // SSM_GROUP_CH
STATE_N = 64
D_FF = ((-(-8 * D_MODEL // 3) + 255) // 256) * 256
IN_WIDTH = 3 * ATTN_WIDTH + SSM_WIDTH
EPS = 1e-6
LAMBDA_RE_MAX = -1e-4

kernel_name = 'hymba_dilated_attn_s5_decoder_step'

F32 = jnp.float32


def rmsnorm(x, g):
    xf = x.astype(F32)
    y = xf * lax.rsqrt(jnp.mean(xf * xf, axis=-1, keepdims=True) + EPS)
    return (y * g.astype(F32)).astype(x.dtype)


def _branch_prompt(q, k, v, window, dil):
    b, s, h, e = q.shape
    n = window // dil
    L = s // dil
    nb = -(-L // n)
    lp = nb * n

    def sub(t):
        t = t.reshape(b, L, dil, h, e).transpose(0, 2, 1, 3, 4)
        t = jnp.pad(t, ((0, 0), (0, 0), (0, lp - L), (0, 0), (0, 0)))
        return t.reshape(b, dil, nb, n, h, e)

    def with_prev(t):
        prev = jnp.pad(t, ((0, 0), (0, 0), (1, 0), (0, 0), (0, 0), (0, 0)))[:, :, :-1]
        return jnp.concatenate([prev, t], axis=3)

    qs = sub(q)
    kk = with_prev(sub(k))
    vv = with_prev(sub(v)).astype(F32)
    scores = jnp.einsum('brcqhe,brckhe->brchqk', qs, kk, preferred_element_type=F32) * (HEAD_DIM ** -0.5)
    qi = jnp.arange(n)[:, None]
    ki = jnp.arange(2 * n)[None, :]
    dist = qi + n - ki
    band = (dist >= 0) & (dist <= n)
    has_prev = (jnp.arange(nb) > 0)[:, None, None] | (ki >= n)[None]
    mask = band[None] & has_prev
    scores = jnp.where(mask[None, None, :, None], scores, -jnp.inf)
    m = jnp.max(scores, axis=-1, keepdims=True)
    p = jnp.exp(scores - m)
    den = jnp.sum(p, axis=-1, keepdims=True)
    lse = (m + jnp.log(den))[..., 0]
    out = jnp.einsum('brchqk,brckhe->brcqhe', p / den, vv)
    out = out.reshape(b, dil, lp, h, e)[:, :, :L].transpose(0, 2, 1, 3, 4).reshape(b, s, h, e)
    lse = lse.transpose(0, 1, 2, 4, 3).reshape(b, dil, lp, h)[:, :, :L].transpose(0, 2, 1, 3).reshape(b, s, h)
    return out, lse


def _branch_sample(q, k_all, v_all, window, dil):
    b, t, h, e = q.shape
    lbuf = k_all.shape[1] - t
    n = window // dil
    idx = lbuf + jnp.arange(t)[:, None] - dil * jnp.arange(n + 1)[None, :]
    valid = idx >= 0
    idx = jnp.maximum(idx, 0)
    kg = k_all[:, idx]
    vg = v_all[:, idx].astype(F32)
    scores = jnp.einsum('bthe,btkhe->bhtk', q, kg, preferred_element_type=F32) * (HEAD_DIM ** -0.5)
    scores = jnp.where(valid[None, None], scores, -jnp.inf)
    m = jnp.max(scores, axis=-1, keepdims=True)
    p = jnp.exp(scores - m)
    den = jnp.sum(p, axis=-1, keepdims=True)
    lse = (m + jnp.log(den))[..., 0].transpose(0, 2, 1)
    out = jnp.einsum('bhtk,btkhe->bthe', p / den, vg)
    return out, lse


def _combine_branches(outs, lses):
    w = jax.nn.softmax(jnp.stack(lses, axis=0), axis=0)
    return jnp.sum(w[..., None] * jnp.stack(outs, axis=0), axis=0)


def _cmul_combine(e1, e2):
    a1r, a1i, b1r, b1i = e1
    a2r, a2i, b2r, b2i = e2
    ar = a1r * a2r - a1i * a2i
    ai = a1r * a2i + a1i * a2r
    br = a2r * b1r - a2i * b1i + b2r
    bi = a2r * b1i + a2i * b1r + b2i
    return ar, ai, br, bi


def _s5(u, h0_re, h0_im, a_re, a_im, log_dt, b_re, b_im, c_re, c_im, d_skip, glu_w, glu_b):
    bsz, t, _ = u.shape
    uf = u.astype(F32).reshape(bsz, t, N_SSM_GROUPS, SSM_GROUP_CH)
    lam_re = jnp.minimum(a_re.astype(F32), LAMBDA_RE_MAX)
    lam_im = a_im.astype(F32)
    dt = jnp.exp(log_dt.astype(F32))[:, None]
    mag = jnp.exp(lam_re * dt)
    ab_re = mag * jnp.cos(lam_im * dt)
    ab_im = mag * jnp.sin(lam_im * dt)
    den = lam_re * lam_re + lam_im * lam_im
    f_re = ((ab_re - 1.0) * lam_re + ab_im * lam_im) / den
    f_im = (ab_im * lam_re - (ab_re - 1.0) * lam_im) / den
    br = b_re.astype(F32)
    bi = b_im.astype(F32)
    bb_re = f_re[..., None] * br - f_im[..., None] * bi
    bb_im = f_re[..., None] * bi + f_im[..., None] * br
    bu_re = jnp.einsum('btgc,gnc->btgn', uf, bb_re)
    bu_im = jnp.einsum('btgc,gnc->btgn', uf, bb_im)
    h0r = h0_re.astype(F32)
    h0i = h0_im.astype(F32)
    bu_re = bu_re.at[:, 0].add(ab_re * h0r - ab_im * h0i)
    bu_im = bu_im.at[:, 0].add(ab_re * h0i + ab_im * h0r)
    a_r = jnp.broadcast_to(ab_re, bu_re.shape)
    a_i = jnp.broadcast_to(ab_im, bu_im.shape)
    _, _, h_re, h_im = lax.associative_scan(_cmul_combine, (a_r, a_i, bu_re, bu_im), axis=1)
    y = (jnp.einsum('btgn,gcn->btgc', h_re, c_re.astype(F32))
         - jnp.einsum('btgn,gcn->btgc', h_im, c_im.astype(F32))
         + d_skip.astype(F32) * uf)
    y = y.reshape(bsz, t, SSM_WIDTH)
    z = jax.nn.gelu(y)
    out = z * jax.nn.sigmoid(z @ glu_w.astype(F32) + glu_b.astype(F32))
    return out, h_re[:, -1], h_im[:, -1]


def _layer(x, k_buf, v_buf, h0_re, h0_im, norm1_g, w_in, attn_out_g, a_re, a_im, log_dt,
           b_re, b_im, c_re, c_im, d_skip, glu_w, glu_b, ssm_out_g, w_out, norm2_g,
           w_gate, w_up, w_down):
    b, t, _ = x.shape
    xn = rmsnorm(x, norm1_g)
    z = xn @ w_in
    q = z[..., :ATTN_WIDTH].reshape(b, t, N_HEADS, HEAD_DIM)
    k = z[..., ATTN_WIDTH:2 * ATTN_WIDTH].reshape(b, t, N_HEADS, HEAD_DIM)
    v = z[..., 2 * ATTN_WIDTH:3 * ATTN_WIDTH].reshape(b, t, N_HEADS, HEAD_DIM)
    u = z[..., 3 * ATTN_WIDTH:]
    if k_buf is None:
        res = [_branch_prompt(q, k, v, w, d) for (w, d) in BRANCHES]
        new_k = k[:, -WIN_MAX:]
        new_v = v[:, -WIN_MAX:]
    else:
        k_all = jnp.concatenate([k_buf.astype(k.dtype), k], axis=1)
        v_all = jnp.concatenate([v_buf.astype(v.dtype), v], axis=1)
        res = [_branch_sample(q, k_all, v_all, w, d) for (w, d) in BRANCHES]
        new_k = k
        new_v = v
    attn = _combine_branches([r[0] for r in res], [r[1] for r in res])
    attn = attn.reshape(b, t, ATTN_WIDTH).astype(x.dtype)
    ssm, h_re, h_im = _s5(u, h0_re, h0_im, a_re, a_im, log_dt, b_re, b_im, c_re, c_im,
                          d_skip, glu_w, glu_b)
    ssm = ssm.astype(x.dtype)
    mixed = jnp.concatenate([rmsnorm(attn, attn_out_g), rmsnorm(ssm, ssm_out_g)], axis=-1) @ w_out
    h = x + mixed
    hn = rmsnorm(h, norm2_g)
    y = h + (jax.nn.silu(hn @ w_gate) * (hn @ w_up)) @ w_down
    return y, new_k, new_v, h_re, h_im


def setup_inputs(seed: int = 0) -> dict:
    key = jax.random.key(seed)
    ks = jax.random.split(key, 32)

    def nrm(k, shape, scale):
        return jax.random.normal(k, shape, F32) * scale

    lbuf = min(WIN_MAX, PAST_LEN)
    G, N, C = N_SSM_GROUPS, STATE_N, SSM_GROUP_CH
    a_im = (jnp.pi * jnp.arange(N, dtype=F32))[None, None, :] + nrm(ks[8], (DEPTH, G, N), 0.01)
    return {
        'x_prompt': nrm(ks[0], (BATCH, SEQ, D_MODEL), 1.0),
        'x_sample': nrm(ks[1], (DEC_BATCH, DEC_SEQ, D_MODEL), 1.0),
        'cache_attn_k': nrm(ks[2], (DEPTH, DEC_BATCH, lbuf, N_HEADS, HEAD_DIM), 1.0),
        'cache_attn_v': nrm(ks[3], (DEPTH, DEC_BATCH, lbuf, N_HEADS, HEAD_DIM), 1.0),
        'state_ssm_re': nrm(ks[4], (DEPTH, DEC_BATCH, G, N), 1.0),
        'state_ssm_im': nrm(ks[5], (DEPTH, DEC_BATCH, G, N), 1.0),
        'norm1_g': 1.0 + nrm(ks[6], (DEPTH, D_MODEL), 0.02),
        'w_in': nrm(ks[7], (DEPTH, D_MODEL, IN_WIDTH), D_MODEL ** -0.5),
        'attn_out_g': 1.0 + nrm(ks[9], (DEPTH, ATTN_WIDTH), 0.02),
        'ssm_a_re': -0.5 + nrm(ks[10], (DEPTH, G, N), 0.01),
        'ssm_a_im': a_im,
        'ssm_log_dt': jax.random.uniform(ks[11], (DEPTH, G), F32, math.log(1e-3), math.log(1e-1)),
        'ssm_b_re': nrm(ks[12], (DEPTH, G, N, C), (2 * C) ** -0.5),
        'ssm_b_im': nrm(ks[13], (DEPTH, G, N, C), (2 * C) ** -0.5),
        'ssm_c_re': nrm(ks[14], (DEPTH, G, C, N), (2 * N) ** -0.5),
        'ssm_c_im': nrm(ks[15], (DEPTH, G, C, N), (2 * N) ** -0.5),
        'ssm_d': nrm(ks[16], (DEPTH, G, C), 1.0),
        'ssm_glu_w': nrm(ks[17], (DEPTH, SSM_WIDTH, SSM_WIDTH), SSM_WIDTH ** -0.5),
        'ssm_glu_b': nrm(ks[18], (DEPTH, SSM_WIDTH), 0.02),
        'ssm_out_g': 1.0 + nrm(ks[19], (DEPTH, SSM_WIDTH), 0.02),
        'w_out': nrm(ks[20], (DEPTH, MIX_WIDTH, D_MODEL), MIX_WIDTH ** -0.5),
        'norm2_g': 1.0 + nrm(ks[21], (DEPTH, D_MODEL), 0.02),
        'ffn_w_gate': nrm(ks[22], (DEPTH, D_MODEL, D_FF), D_MODEL ** -0.5),
        'ffn_w_up': nrm(ks[23], (DEPTH, D_MODEL, D_FF), D_MODEL ** -0.5),
        'ffn_w_down': nrm(ks[24], (DEPTH, D_FF, D_MODEL), D_FF ** -0.5),
        'final_norm_g': 1.0 + nrm(ks[25], (D_MODEL,), 0.02),
    }


def reference(x_prompt, x_sample, cache_attn_k, cache_attn_v, state_ssm_re, state_ssm_im,
              norm1_g, w_in, attn_out_g, ssm_a_re, ssm_a_im, ssm_log_dt, ssm_b_re, ssm_b_im,
              ssm_c_re, ssm_c_im, ssm_d, ssm_glu_w, ssm_glu_b, ssm_out_g, w_out, norm2_g,
              ffn_w_gate, ffn_w_up, ffn_w_down, final_norm_g):
    xp = x_prompt
    xs = x_sample
    zeros_p = jnp.zeros((x_prompt.shape[0], N_SSM_GROUPS, STATE_N), F32)
    pk, pv, pr, pim = [], [], [], []
    sk, sv, sr, sim = [], [], [], []
    for l in range(DEPTH):
        lw = (norm1_g[l], w_in[l], attn_out_g[l], ssm_a_re[l], ssm_a_im[l], ssm_log_dt[l],
              ssm_b_re[l], ssm_b_im[l], ssm_c_re[l], ssm_c_im[l], ssm_d[l], ssm_glu_w[l],
              ssm_glu_b[l], ssm_out_g[l], w_out[l], norm2_g[l], ffn_w_gate[l], ffn_w_up[l],
              ffn_w_down[l])
        xp, k_p, v_p, r_p, i_p = _layer(xp, None, None, zeros_p, zeros_p, *lw)
        pk.append(k_p)
        pv.append(v_p)
        pr.append(r_p)
        pim.append(i_p)
        xs, k_s, v_s, r_s, i_s = _layer(xs, cache_attn_k[l], cache_attn_v[l],
                                        state_ssm_re[l], state_ssm_im[l], *lw)
        sk.append(k_s)
        sv.append(v_s)
        sr.append(r_s)
        sim.append(i_s)
    y_prompt = rmsnorm(xp, final_norm_g)
    y_sample = rmsnorm(xs, final_norm_g)
    return (y_prompt, y_sample,
            jnp.stack(pk), jnp.stack(pv), jnp.stack(pr), jnp.stack(pim),
            jnp.stack(sk), jnp.stack(sv), jnp.stack(sr), jnp.stack(sim))
```

```python
import functools

import numpy as np
import jax
import jax.numpy as jnp
from jax import lax
from jax.experimental import pallas as pl
from jax.experimental.pallas import tpu as pltpu

F32 = jnp.float32
BF16 = jnp.bfloat16

HEAD_DIM = 64
BRANCHES = ((128, 1), (512, 4), (2048, 16))
BRANCH_DIL = tuple(d for _, d in BRANCHES)
BAND = 128
assert all(w // d == BAND for w, d in BRANCHES)
SSM_GROUP_CH = 16
STATE_N = 64
EPS = 1e-6
LAMBDA_RE_MAX = -1e-4

LANES = 128
VREG_ELEMS = 8 * LANES
VMEM_LIMIT = 56 * 1024 * 1024


def _resident(shape, index_map):
    return pl.BlockSpec(shape, index_map, pipeline_mode=pl.Buffered(1))


def _params(n_axes):
    return pltpu.CompilerParams(dimension_semantics=("arbitrary",) * n_axes,
                                vmem_limit_bytes=VMEM_LIMIT)


def _rms(x, g):
    return x * lax.rsqrt(jnp.mean(x * x, axis=-1, keepdims=True) + EPS) * g


def _nt(a, b):
    return lax.dot_general(a, b, (((1,), (1,)), ((), ())), preferred_element_type=F32)


def _dot(a, b):
    return jnp.dot(a, b, preferred_element_type=F32)


def _in_proj_kernel(x_ref, g_ref, w_ref, q_ref, k_ref, v_ref, u_ref):
    xn = _rms(x_ref[...], g_ref[0])
    z = _dot(xn.astype(BF16), w_ref[0])
    aw = q_ref.shape[-1]
    q_ref[...] = z[:, :aw]
    k_ref[...] = z[:, aw:2 * aw]
    v_ref[...] = z[:, 2 * aw:3 * aw]
    u_ref[...] = z[:, 3 * aw:]


def _in_proj(x, g, w, layer, tm):
    m, dm = x.shape
    n = w.shape[-1]
    sw = g.shape[-1] // 2
    aw = (n - sw) // 3
    row = lambda width: pl.BlockSpec((tm, width), lambda i: (i, 0))
    return pl.pallas_call(
        _in_proj_kernel,
        grid=(m // tm,),
        in_specs=[row(dm),
                  _resident((1, 1, dm), lambda i: (layer, 0, 0)),
                  _resident((1, dm, n), lambda i: (layer, 0, 0))],
        out_specs=[row(aw), row(aw), row(aw), row(sw)],
        out_shape=[jax.ShapeDtypeStruct((m, aw), F32)] * 3 + [jax.ShapeDtypeStruct((m, sw), F32)],
        compiler_params=_params(1),
        name="in_proj",
    )(x, g, w)


def _attn_prompt_kernel(q_ref, k_ref, v_ref, o_ref, qlo, qhi, kb, vlo, vhi, acc_s, m_s, l_s, *, seq):
    nblk = seq // BAND
    lane = lax.broadcasted_iota(jnp.int32, (BAND, LANES), 1)
    lo = lane < HEAD_DIM
    row = lax.broadcasted_iota(jnp.int32, (BAND, BAND), 0)
    col = lax.broadcasted_iota(jnp.int32, (BAND, BAND), 1)
    causal = col <= row
    anti = col >= row
    zero = jnp.zeros((BAND, LANES), F32)

    for bi, d in enumerate(BRANCH_DIL):
        nb = nblk // d

        def natural(idx, d=d, nb=nb):
            r = idx // nb
            c = idx % nb
            st = r + d * BAND * c
            return (pl.ds(st, BAND, stride=d) if d > 1 else pl.ds(st, BAND)), c

        def packed(idx):
            return pl.ds(pl.multiple_of(idx * BAND, BAND), BAND)

        def stage(idx, carry, bi=bi, natural=natural):
            src, _ = natural(idx)
            dst = packed(idx)
            qv = q_ref[0, src, :] * (HEAD_DIM ** -0.5)
            qlo[bi, dst, :] = jnp.where(lo, qv, zero).astype(BF16)
            qhi[bi, dst, :] = jnp.where(lo, zero, qv).astype(BF16)
            kb[bi, dst, :] = k_ref[0, src, :].astype(BF16)
            vv = v_ref[0, src, :]
            vlo[bi, dst, :] = jnp.where(lo, vv, zero).astype(BF16)
            vhi[bi, dst, :] = jnp.where(lo, zero, vv).astype(BF16)
            return carry

        lax.fori_loop(0, nblk, stage, 0)

        def block(idx, carry, bi=bi, nb=nb, natural=natural):
            dst, c = natural(idx)
            cur = packed(idx)
            k_cur = kb[bi, cur, :]
            if nb > 1:
                prv = packed(jnp.maximum(idx - 1, 0))
                k_prev = kb[bi, prv, :]
                prev_ok = jnp.logical_and(anti, c > 0)
            accs, ms, ls = [], [], []
            for qr, vr in ((qlo, vlo), (qhi, vhi)):
                qb = qr[bi, cur, :]
                s_cur = jnp.where(causal, _nt(qb, k_cur), -jnp.inf)
                m = jnp.max(s_cur, axis=-1, keepdims=True)
                if nb > 1:
                    s_prev = jnp.where(prev_ok, _nt(qb, k_prev), -jnp.inf)
                    m = jnp.maximum(m, jnp.max(s_prev, axis=-1, keepdims=True))
                p_cur = jnp.exp(s_cur - m)
                l = jnp.sum(p_cur, axis=-1, keepdims=True)
                acc = _dot(p_cur.astype(BF16), vr[bi, cur, :])
                if nb > 1:
                    p_prev = jnp.exp(s_prev - m)
                    l = l + jnp.sum(p_prev, axis=-1, keepdims=True)
                    acc = acc + _dot(p_prev.astype(BF16), vr[bi, prv, :])
                accs.append(acc)
                ms.append(m)
                ls.append(l)
            acc_s[bi, dst, :] = accs[0] + accs[1]
            m_s[bi, dst, :] = jnp.where(lo, ms[0], ms[1])
            l_s[bi, dst, :] = jnp.where(lo, ls[0], ls[1])
            return carry

        lax.fori_loop(0, nblk, block, 0)

    def combine(i, carry):
        sl = pl.ds(pl.multiple_of(i * BAND, BAND), BAND)
        ms = [m_s[bi, sl, :] for bi in range(len(BRANCH_DIL))]
        mm = functools.reduce(jnp.maximum, ms)
        fs = [jnp.exp(m - mm) for m in ms]
        num = sum(f * acc_s[bi, sl, :] for bi, f in enumerate(fs))
        den = sum(f * l_s[bi, sl, :] for bi, f in enumerate(fs))
        o_ref[0, sl, :] = num / den
        return carry

    lax.fori_loop(0, nblk, combine, 0)


def _attn_prompt(q, k, v):
    b, s, w = q.shape
    assert s % (BAND * max(BRANCH_DIL)) == 0 and w % LANES == 0
    spec = pl.BlockSpec((1, s, LANES), lambda i, j: (i, 0, j))
    nbr = len(BRANCH_DIL)
    return pl.pallas_call(
        functools.partial(_attn_prompt_kernel, seq=s),
        grid=(b, w // LANES),
        in_specs=[spec, spec, spec],
        out_specs=spec,
        out_shape=jax.ShapeDtypeStruct((b, s, w), F32),
        scratch_shapes=[pltpu.VMEM((nbr, s, LANES), BF16)] * 5 + [pltpu.VMEM((nbr, s, LANES), F32)] * 3,
        compiler_params=_params(2),
        name="attn_prompt",
    )(q, k, v)


def _sample_multiplicity(lbuf, t):
    dist_c = lbuf + np.arange(t)[:, None] - np.arange(lbuf)[None, :]
    dist_n = np.arange(t)[:, None] - np.arange(t)[None, :]
    mc = np.zeros((t, lbuf), np.float32)
    mn = np.zeros((t, t), np.float32)
    for w, d in BRANCHES:
        mc += (dist_c % d == 0) & (dist_c <= w)
        mn += (dist_n >= 0) & (dist_n % d == 0) & (dist_n <= w)
    return mc, mn


def _attn_sample_kernel(q_ref, kn_ref, vn_ref, kc_ref, vc_ref, mc_ref, mn_ref, o_ref):
    n_heads = kc_ref.shape[2]
    mc = mc_ref[...]
    mn = mn_ref[...]
    outs = []
    for h in range(n_heads):
        hs = slice(h * HEAD_DIM, (h + 1) * HEAD_DIM)
        qh = (q_ref[:, hs] * (HEAD_DIM ** -0.5)).astype(BF16)
        s_c = jnp.where(mc > 0, _dot(qh, kc_ref[0, 0, h].astype(BF16)), -jnp.inf)
        s_n = jnp.where(mn > 0, _nt(qh, kn_ref[:, hs].astype(BF16)), -jnp.inf)
        m = jnp.maximum(jnp.max(s_c, axis=-1, keepdims=True), jnp.max(s_n, axis=-1, keepdims=True))
        p_c = mc * jnp.exp(s_c - m)
        p_n = mn * jnp.exp(s_n - m)
        den = jnp.sum(p_c, axis=-1, keepdims=True) + jnp.sum(p_n, axis=-1, keepdims=True)
        num = (_nt(p_c.astype(BF16), vc_ref[0, 0, h].astype(BF16))
               + _dot(p_n.astype(BF16), vn_ref[:, hs].astype(BF16)))
        outs.append(num / den)
    o_ref[...] = jnp.concatenate(outs, axis=-1)


def _attn_sample(q, kn, vn, kc, vc, layer, t):
    m, w = q.shape
    _, b, h, e, lbuf = kc.shape
    for win, d in BRANCHES:
        assert lbuf - d * (win // d) >= 0
    mc, mn = _sample_multiplicity(lbuf, t)
    row = pl.BlockSpec((t, w), lambda i: (i, 0))
    cache = pl.BlockSpec((1, 1, h, e, lbuf), lambda i: (layer, i, 0, 0, 0))
    return pl.pallas_call(
        _attn_sample_kernel,
        grid=(b,),
        in_specs=[row, row, row, cache, cache,
                  _resident((t, lbuf), lambda i: (0, 0)), _resident((t, t), lambda i: (0, 0))],
        out_specs=row,
        out_shape=jax.ShapeDtypeStruct((m, w), F32),
        compiler_params=_params(1),
        name="attn_sample",
    )(q, kn, vn, kc, vc, jnp.asarray(mc), jnp.asarray(mn))


def _s5_prep_kernel(are_ref, aim_ref, ldt_ref, br_ref, bi_ref, abr_ref, abi_ref, bbr_ref, bbi_ref):
    lam_re = jnp.minimum(are_ref[0], LAMBDA_RE_MAX)
    lam_im = aim_ref[0]
    dt = jnp.exp(ldt_ref[0])
    mag = jnp.exp(lam_re * dt)
    ab_re = mag * jnp.cos(lam_im * dt)
    ab_im = mag * jnp.sin(lam_im * dt)
    den = lam_re * lam_re + lam_im * lam_im
    f_re = ((ab_re - 1.0) * lam_re + ab_im * lam_im) / den
    f_im = (ab_im * lam_re - (ab_re - 1.0) * lam_im) / den
    abr_ref[0] = ab_re
    abi_ref[0] = ab_im
    br = br_ref[0]
    bi = bi_ref[0]
    bbr_ref[0] = (f_re * br - f_im * bi).astype(BF16)
    bbi_ref[0] = (f_re * bi + f_im * br).astype(BF16)


def _s5_prep(a_re, a_im, log_dt, b_re_blk, b_im_blk):
    nl, sw, gn = b_re_blk.shape
    vec = pl.BlockSpec((1, 1, gn), lambda l: (l, 0, 0))
    mat = pl.BlockSpec((1, sw, gn), lambda l: (l, 0, 0))
    return pl.pallas_call(
        _s5_prep_kernel,
        grid=(nl,),
        in_specs=[vec, vec, vec, mat, mat],
        out_specs=[vec, vec, mat, mat],
        out_shape=[jax.ShapeDtypeStruct((nl, 1, gn), F32)] * 2 + [jax.ShapeDtypeStruct((nl, sw, gn), BF16)] * 2,
        compiler_params=_params(1),
        name="s5_prep",
    )(a_re, a_im, log_dt, b_re_blk, b_im_blk)


def _s5_kernel(u_ref, h0r_ref, h0i_ref, abr_ref, abi_ref, bbr_ref, bbi_ref, cr_ref, ci_ref, d_ref,
               gw_ref, gb_ref, y_ref, hr_ref, hi_ref, sr, si):
    nb, tc, sw = u_ref.shape
    gn = abr_ref.shape[-1]

    @pl.when(pl.program_id(0) == 0)
    def _():
        hr_ref[...] = h0r_ref[0]
        hi_ref[...] = h0i_ref[0]

    u = u_ref[...].reshape(nb * tc, sw)
    ub = u.astype(BF16)
    n_col = gn // LANES
    bur = _dot(ub, bbr_ref[0])
    bui = _dot(ub, bbi_ref[0])
    for j in range(n_col):
        sr[j] = bur[:, j * LANES:(j + 1) * LANES]
        si[j] = bui[:, j * LANES:(j + 1) * LANES]

    group = max(1, min(n_col, 8 * 8 // nb))
    for j0 in range(0, n_col, group):
        js = range(j0, j0 + group)
        lanes = [slice(j * LANES, (j + 1) * LANES) for j in js]
        ar = [jnp.broadcast_to(abr_ref[0, :, c], (nb, LANES)) for c in lanes]
        ai = [jnp.broadcast_to(abi_ref[0, :, c], (nb, LANES)) for c in lanes]

        def step(t, h, js=js, ar=ar, ai=ai):
            rows = pl.ds(t, nb, stride=tc)
            out = []
            for j, a_r, a_i, (hr, hi) in zip(js, ar, ai, h):
                nr = a_r * hr - a_i * hi + sr[j, rows, :]
                ni = a_r * hi + a_i * hr + si[j, rows, :]
                sr[j, rows, :] = nr
                si[j, rows, :] = ni
                out.append((nr, ni))
            return tuple(out)

        h = lax.fori_loop(0, tc, step, tuple((hr_ref[:, c], hi_ref[:, c]) for c in lanes))
        for c, (hr, hi) in zip(lanes, h):
            hr_ref[:, c] = hr
            hi_ref[:, c] = hi

    h_re = jnp.concatenate([sr[j] for j in range(n_col)], axis=-1).astype(BF16)
    h_im = jnp.concatenate([si[j] for j in range(n_col)], axis=-1).astype(BF16)
    y = _dot(h_re, cr_ref[0]) - _dot(h_im, ci_ref[0]) + d_ref[0] * u
    z = jax.nn.gelu(y)
    out = z * jax.nn.sigmoid(_dot(z.astype(BF16), gw_ref[0]) + gb_ref[0])
    y_ref[...] = out.reshape(nb, tc, sw)


def _s5(u, h0r, h0i, abr, abi, bbr, bbi, cr, ci, dskip, gw, gb, layer, h0_layer, tc):
    nb, t, sw = u.shape
    gn = abr.shape[-1]
    lay = lambda *shape: _resident((1,) + shape, lambda i: (layer,) + (0,) * len(shape))
    h0 = _resident((1, nb, gn), lambda i: (h0_layer, 0, 0))
    state = pl.BlockSpec((nb, gn), lambda i: (0, 0))
    blk = pl.BlockSpec((nb, tc, sw), lambda i: (0, i, 0))
    return pl.pallas_call(
        _s5_kernel,
        grid=(t // tc,),
        in_specs=[blk, h0, h0, lay(1, gn), lay(1, gn), lay(sw, gn), lay(sw, gn), lay(gn, sw), lay(gn, sw),
                  lay(1, sw), lay(sw, sw), lay(1, sw)],
        out_specs=[blk, state, state],
        out_shape=[jax.ShapeDtypeStruct((nb, t, sw), F32)] + [jax.ShapeDtypeStruct((nb, gn), F32)] * 2,
        scratch_shapes=[pltpu.VMEM((gn // LANES, nb * tc, LANES), F32)] * 2,
        compiler_params=_params(1),
        name="s5",
    )(u, h0r, h0i, abr, abi, bbr, bbi, cr, ci, dskip, gw, gb)


def _mix_ffn_kernel(x_ref, a_ref, s_ref, ag_ref, sg_ref, wo_ref, n2_ref, wg_ref, wu_ref, wd_ref, fg_ref,
                    y_ref, *, ff_chunk, final):
    aw = a_ref.shape[-1]
    an = _rms(a_ref[...], ag_ref[0]).astype(BF16)
    sn = _rms(s_ref[...], sg_ref[0]).astype(BF16)
    h = x_ref[...] + (_dot(an, wo_ref[0, :aw, :]) + _dot(sn, wo_ref[0, aw:, :]))
    hn = _rms(h, n2_ref[0]).astype(BF16)
    ffn = None
    for c0 in range(0, wg_ref.shape[-1], ff_chunk):
        cols = slice(c0, c0 + ff_chunk)
        act = jax.nn.silu(_dot(hn, wg_ref[0, :, cols])) * _dot(hn, wu_ref[0, :, cols])
        part = _dot(act.astype(BF16), wd_ref[0, cols, :])
        ffn = part if ffn is None else ffn + part
    y = h + ffn
    if final:
        y = _rms(y, fg_ref[...])
    y_ref[...] = y


def _mix_ffn(x, attn, ssm, ag, sg, wo, n2, wg, wu, wd, fg, layer, tm, final):
    m, dm = x.shape
    aw, sw, dff = attn.shape[-1], ssm.shape[-1], wg.shape[-1]
    ff_chunk = dff // 2
    assert ff_chunk % LANES == 0
    row = lambda width: pl.BlockSpec((tm, width), lambda i: (i, 0))
    lay = lambda *shape: _resident((1,) + shape, lambda i: (layer,) + (0,) * len(shape))
    return pl.pallas_call(
        functools.partial(_mix_ffn_kernel, ff_chunk=ff_chunk, final=final),
        grid=(m // tm,),
        in_specs=[row(dm), row(aw), row(sw), lay(1, aw), lay(1, sw), lay(aw + sw, dm), lay(1, dm),
                  lay(dm, dff), lay(dm, dff), lay(dff, dm), _resident((1, dm), lambda i: (0, 0))],
        out_specs=row(dm),
        out_shape=jax.ShapeDtypeStruct((m, dm), F32),
        compiler_params=_params(1),
        name="mix_ffn",
    )(x, attn, ssm, ag, sg, wo, n2, wg, wu, wd, fg)


def _block_diag(w, eye):
    nl, g, p, q = w.shape
    return jnp.einsum("lgpq,gh->lgphq", w, eye).reshape(nl, g * p, g * q)


def kernel(x_prompt, x_sample, cache_attn_k, cache_attn_v, state_ssm_re, state_ssm_im, norm1_g, w_in, attn_out_g, ssm_a_re, ssm_a_im, ssm_log_dt, ssm_b_re, ssm_b_im, ssm_c_re, ssm_c_im, ssm_d, ssm_glu_w, ssm_glu_b, ssm_out_g, w_out, norm2_g, ffn_w_gate, ffn_w_up, ffn_w_down, final_norm_g):
    bp, seq, dm = x_prompt.shape
    bs, tdec, _ = x_sample.shape
    depth, _, lbuf, n_heads, head_dim = cache_attn_k.shape
    _, n_groups, state_n = ssm_a_re.shape
    gn = n_groups * state_n
    aw = n_heads * head_dim
    sw = n_groups * SSM_GROUP_CH
    assert head_dim == HEAD_DIM and state_n == STATE_N and lbuf == min(max(w for w, _ in BRANCHES), lbuf)

    vec = lambda a: a.reshape(depth, 1, -1)
    w_in_b, w_out_b = w_in.astype(BF16), w_out.astype(BF16)
    wg_b, wu_b, wd_b = ffn_w_gate.astype(BF16), ffn_w_up.astype(BF16), ffn_w_down.astype(BF16)
    glu_w_b = ssm_glu_w.astype(BF16)
    eye = jnp.eye(n_groups, dtype=F32)
    b_re_blk = _block_diag(jnp.swapaxes(ssm_b_re, 2, 3), eye)
    b_im_blk = _block_diag(jnp.swapaxes(ssm_b_im, 2, 3), eye)
    c_re_blk = _block_diag(jnp.swapaxes(ssm_c_re, 2, 3), eye).astype(BF16)
    c_im_blk = _block_diag(jnp.swapaxes(ssm_c_im, 2, 3), eye).astype(BF16)
    log_dt = jnp.repeat(ssm_log_dt, state_n, axis=-1)
    abr, abi, bbr, bbi = _s5_prep(vec(ssm_a_re), vec(ssm_a_im), vec(log_dt), b_re_blk, b_im_blk)
    n1, n2, ag, sg = vec(norm1_g), vec(norm2_g), vec(attn_out_g), vec(ssm_out_g)
    dsk, gb = vec(ssm_d), vec(ssm_glu_b)
    fg = final_norm_g.reshape(1, dm)
    kc = jnp.transpose(cache_attn_k, (0, 1, 3, 4, 2))
    vc = jnp.transpose(cache_attn_v, (0, 1, 3, 4, 2))
    h0p = jnp.zeros((1, bp, gn), F32)
    h0s_re = state_ssm_re.reshape(depth, bs, gn)
    h0s_im = state_ssm_im.reshape(depth, bs, gn)

    xp = x_prompt.reshape(bp * seq, dm)
    xs = x_sample.reshape(bs * tdec, dm)
    pk, pv, pr, pim, sk, sv, sr, sim = ([] for _ in range(8))
    for l in range(depth):
        final = l == depth - 1
        s5w = (abr, abi, bbr, bbi, c_re_blk, c_im_blk, dsk, glu_w_b, gb)
        ffw = (ag, sg, w_out_b, n2, wg_b, wu_b, wd_b, fg)

        q, k, v, u = _in_proj(xp, n1, w_in_b, l, tm=512)
        attn = _attn_prompt(*(t.reshape(bp, seq, aw) for t in (q, k, v))).reshape(bp * seq, aw)
        ssm, h_re, h_im = _s5(u.reshape(bp, seq, sw), h0p, h0p, *s5w, layer=l, h0_layer=0, tc=64)
        xp = _mix_ffn(xp, attn, ssm.reshape(bp * seq, sw), *ffw, layer=l, tm=512, final=final)
        pk.append(k.reshape(bp, seq, n_heads, head_dim)[:, -lbuf:])
        pv.append(v.reshape(bp, seq, n_heads, head_dim)[:, -lbuf:])
        pr.append(h_re.reshape(bp, n_groups, state_n))
        pim.append(h_im.reshape(bp, n_groups, state_n))

        q, k, v, u = _in_proj(xs, n1, w_in_b, l, tm=bs * tdec)
        attn = _attn_sample(q, k, v, kc, vc, l, tdec)
        ssm, h_re, h_im = _s5(u.reshape(bs, tdec, sw), h0s_re, h0s_im, *s5w, layer=l, h0_layer=l, tc=tdec)
        xs = _mix_ffn(xs, attn, ssm.reshape(bs * tdec, sw), *ffw, layer=l, tm=bs * tdec, final=final)
        sk.append(k.reshape(bs, tdec, n_heads, head_dim))
        sv.append(v.reshape(bs, tdec, n_heads, head_dim))
        sr.append(h_re.reshape(bs, n_groups, state_n))
        sim.append(h_im.reshape(bs, n_groups, state_n))

    return (xp.reshape(bp, seq, dm), xs.reshape(bs, tdec, dm),
            jnp.stack(pk), jnp.stack(pv), jnp.stack(pr), jnp.stack(pim),
            jnp.stack(sk), jnp.stack(sv), jnp.stack(sr), jnp.stack(sim))
```

```python
import functools

import numpy as np
import jax
import jax.numpy as jnp
from jax import lax
from jax.experimental import pallas as pl
from jax.experimental.pallas import tpu as pltpu

F32 = jnp.float32
BF16 = jnp.bfloat16

HEAD_DIM = 64
BRANCHES = ((128, 1), (512, 4), (2048, 16))
BRANCH_DIL = tuple(d for _, d in BRANCHES)
BAND = 128
assert all(w // d == BAND for w, d in BRANCHES)
SSM_GROUP_CH = 16
STATE_N = 64
EPS = 1e-6
LAMBDA_RE_MAX = -1e-4

LANES = 128
SUBLANES = 8
VMEM_LIMIT = 56 * 1024 * 1024

ROW_TILE = 512
SCAN_CHUNK = 128
SCAN_VREGS = 8


def _resident(shape, index_map):
    return pl.BlockSpec(shape, index_map, pipeline_mode=pl.Buffered(1))


def _params(n_axes):
    return pltpu.CompilerParams(dimension_semantics=("arbitrary",) * n_axes,
                                vmem_limit_bytes=VMEM_LIMIT)


def _rms(x, g):
    return x * lax.rsqrt(jnp.mean(x * x, axis=-1, keepdims=True) + EPS) * g


def _nt(a, b):
    return lax.dot_general(a, b, (((1,), (1,)), ((), ())), preferred_element_type=F32)


def _dot(a, b):
    return jnp.dot(a, b, preferred_element_type=F32)


def _in_proj_kernel(x_ref, g_ref, w_ref, q_ref, k_ref, v_ref, u_ref):
    xn = _rms(x_ref[...], g_ref[0])
    z = _dot(xn.astype(BF16), w_ref[0])
    aw = q_ref.shape[-1]
    q_ref[...] = z[:, :aw]
    k_ref[...] = z[:, aw:2 * aw]
    v_ref[...] = z[:, 2 * aw:3 * aw]
    u_ref[...] = z[:, 3 * aw:]


def _in_proj(x, g, w, layer):
    m, dm = x.shape
    n = w.shape[-1]
    sw = g.shape[-1] // 2
    aw = (n - sw) // 3
    tm = min(m, ROW_TILE)
    row = lambda width: pl.BlockSpec((tm, width), lambda i: (i, 0))
    return pl.pallas_call(
        _in_proj_kernel,
        grid=(m // tm,),
        in_specs=[row(dm),
                  _resident((1, 1, dm), lambda i: (layer, 0, 0)),
                  _resident((1, dm, n), lambda i: (layer, 0, 0))],
        out_specs=[row(aw), row(aw), row(aw), row(sw)],
        out_shape=[jax.ShapeDtypeStruct((m, aw), F32)] * 3 + [jax.ShapeDtypeStruct((m, sw), F32)],
        compiler_params=_params(1),
        name="in_proj",
    )(x, g, w)


def _attn_prompt_kernel(q_ref, k_ref, v_ref, o_ref, qlo, qhi, kb, vlo, vhi, a_s, b_s, m_s, s_ring, p_ring, *, seq):
    nblk = seq // BAND
    lane = lax.broadcasted_iota(jnp.int32, (BAND, LANES), 1)
    lo = lane < HEAD_DIM
    row = lax.broadcasted_iota(jnp.int32, (BAND, BAND), 0)
    col = lax.broadcasted_iota(jnp.int32, (BAND, BAND), 1)
    causal = col <= row
    anti = col >= row
    zero = jnp.zeros((BAND, LANES), F32)
    one = jnp.ones((BAND, LANES), F32)

    for bi, d in enumerate(BRANCH_DIL):
        nb = nblk // d

        def natural(idx, d=d, nb=nb):
            r = idx // nb
            c = idx % nb
            st = r + d * BAND * c
            return (pl.ds(st, BAND, stride=d) if d > 1 else pl.ds(pl.multiple_of(st, BAND), BAND)), c

        def packed(idx):
            return pl.ds(pl.multiple_of(idx * BAND, BAND), BAND)

        def stage(idx, carry, bi=bi, natural=natural):
            src, _ = natural(idx)
            dst = packed(idx)
            qv = q_ref[src, :] * (HEAD_DIM ** -0.5)
            qlo[bi, dst, :] = jnp.where(lo, qv, zero).astype(BF16)
            qhi[bi, dst, :] = jnp.where(lo, zero, qv).astype(BF16)
            kb[bi, dst, :] = k_ref[src, :].astype(BF16)
            vv = v_ref[src, :]
            vlo[bi, dst, :] = jnp.where(lo, vv, one).astype(BF16)
            vhi[bi, dst, :] = jnp.where(lo, one, vv).astype(BF16)
            return carry

        lax.fori_loop(0, nblk, stage, 0)

        def scores(idx, slot, bi=bi, nb=nb):
            cur = packed(idx)
            k_cur = kb[bi, cur, :]
            if nb > 1:
                k_prev = kb[bi, packed(jnp.maximum(idx - 1, 0)), :]
            for h, qr in enumerate((qlo, qhi)):
                qb = qr[bi, cur, :]
                s_ring[slot, h, :, :BAND] = _nt(qb, k_cur)
                if nb > 1:
                    s_ring[slot, h, :, BAND:] = _nt(qb, k_prev)

        def softmax(idx, slot, bi=bi, nb=nb, natural=natural):
            dst, c = natural(idx)
            if nb > 1:
                prev_ok = jnp.logical_and(anti, c > 0)
            ms = []
            for h in range(2):
                s_cur = jnp.where(causal, s_ring[slot, h, :, :BAND], -jnp.inf)
                if nb > 1:
                    s_prev = jnp.where(prev_ok, s_ring[slot, h, :, BAND:], -jnp.inf)
                    m = jnp.max(jnp.maximum(s_cur, s_prev), axis=-1, keepdims=True)
                    p_ring[slot, h, :, BAND:] = jnp.exp(s_prev - m).astype(BF16)
                else:
                    m = jnp.max(s_cur, axis=-1, keepdims=True)
                p_ring[slot, h, :, :BAND] = jnp.exp(s_cur - m).astype(BF16)
                ms.append(m)
            m_s[bi, dst, :] = jnp.where(lo, ms[0], ms[1])

        def values(idx, slot, bi=bi, nb=nb, natural=natural):
            dst, _ = natural(idx)
            cur = packed(idx)
            accs = []
            for h, vr in enumerate((vlo, vhi)):
                acc = _dot(p_ring[slot, h, :, :BAND], vr[bi, cur, :])
                if nb > 1:
                    prv = packed(jnp.maximum(idx - 1, 0))
                    acc = acc + _dot(p_ring[slot, h, :, BAND:], vr[bi, prv, :])
                accs.append(acc)
            a_s[bi, dst, :] = jnp.where(lo, accs[0], accs[1])
            b_s[bi, dst, :] = jnp.where(lo, accs[1], accs[0])

        scores(0, 0)
        scores(1, 1)
        softmax(0, 0)

        def trip(i, carry, scores=scores, softmax=softmax, values=values):
            j = 2 * i
            values(j - 2, 0)
            softmax(j - 1, 1)
            scores(j, 0)
            values(j - 1, 1)
            softmax(j, 0)
            scores(j + 1, 1)
            return carry

        lax.fori_loop(1, nblk // 2, trip, 0)
        values(nblk - 2, 0)
        softmax(nblk - 1, 1)
        values(nblk - 1, 1)

    def combine(i, carry):
        sl = pl.ds(pl.multiple_of(i * BAND, BAND), BAND)
        ms = [m_s[bi, sl, :] for bi in range(len(BRANCH_DIL))]
        mm = functools.reduce(jnp.maximum, ms)
        fs = [jnp.exp(m - mm) for m in ms]
        num = sum(f * a_s[bi, sl, :] for bi, f in enumerate(fs))
        den_swapped = sum(pltpu.roll(f, HEAD_DIM, axis=1) * b_s[bi, sl, :] for bi, f in enumerate(fs))
        o_ref[sl, :] = num / pltpu.roll(den_swapped, HEAD_DIM, axis=1)
        return carry

    lax.fori_loop(0, nblk, combine, 0, unroll=2)


def _attn_prompt(q, k, v, seq):
    m, w = q.shape
    b = m // seq
    assert seq % (BAND * max(BRANCH_DIL)) == 0 and w % LANES == 0
    pairs = w // LANES
    spec = pl.BlockSpec((seq, LANES), lambda i, j: (0, i * pairs + j))
    nbr = len(BRANCH_DIL)
    out = pl.pallas_call(
        functools.partial(_attn_prompt_kernel, seq=seq),
        grid=(b, pairs),
        in_specs=[spec, spec, spec],
        out_specs=spec,
        out_shape=jax.ShapeDtypeStruct((seq, b * w), F32),
        scratch_shapes=([pltpu.VMEM((nbr, seq, LANES), BF16)] * 5 + [pltpu.VMEM((nbr, seq, LANES), F32)] * 3
                        + [pltpu.VMEM((2, 2, BAND, 2 * BAND), F32), pltpu.VMEM((2, 2, BAND, 2 * BAND), BF16)]),
        compiler_params=_params(2),
        name="attn_prompt",
    )(*(t.reshape(seq, b * w) for t in (q, k, v)))
    return out.reshape(m, w)


def _sample_multiplicity(lbuf, t):
    dist_c = lbuf + np.arange(t)[:, None] - np.arange(lbuf)[None, :]
    dist_n = np.arange(t)[:, None] - np.arange(t)[None, :]
    mc = np.zeros((t, lbuf), np.float32)
    mn = np.zeros((t, t), np.float32)
    for w, d in BRANCHES:
        mc += (dist_c % d == 0) & (dist_c <= w)
        mn += (dist_n >= 0) & (dist_n % d == 0) & (dist_n <= w)
    return mc, mn


def _attn_sample_kernel(q_ref, kn_ref, vn_ref, kc_ref, vc_ref, mc_ref, mn_ref, o_ref):
    n_heads = kc_ref.shape[2]
    mc = mc_ref[...]
    mn = mn_ref[...]
    outs = []
    for h in range(n_heads):
        hs = slice(h * HEAD_DIM, (h + 1) * HEAD_DIM)
        qh = (q_ref[:, hs] * (HEAD_DIM ** -0.5)).astype(BF16)
        s_c = jnp.where(mc > 0, _dot(qh, kc_ref[0, 0, h].astype(BF16)), -jnp.inf)
        s_n = jnp.where(mn > 0, _nt(qh, kn_ref[:, hs].astype(BF16)), -jnp.inf)
        m = jnp.maximum(jnp.max(s_c, axis=-1, keepdims=True), jnp.max(s_n, axis=-1, keepdims=True))
        p_c = mc * jnp.exp(s_c - m)
        p_n = mn * jnp.exp(s_n - m)
        den = jnp.sum(p_c, axis=-1, keepdims=True) + jnp.sum(p_n, axis=-1, keepdims=True)
        num = (_nt(p_c.astype(BF16), vc_ref[0, 0, h].astype(BF16))
               + _dot(p_n.astype(BF16), vn_ref[:, hs].astype(BF16)))
        outs.append(num / den)
    o_ref[...] = jnp.concatenate(outs, axis=-1)


def _attn_sample(q, kn, vn, kc, vc, layer, t):
    m, w = q.shape
    _, b, h, e, lbuf = kc.shape
    for win, d in BRANCHES:
        assert lbuf - d * (win // d) >= 0
    mc, mn = _sample_multiplicity(lbuf, t)
    row = pl.BlockSpec((t, w), lambda i: (0, i))
    cache = pl.BlockSpec((1, 1, h, e, lbuf), lambda i: (layer, i, 0, 0, 0))
    out = pl.pallas_call(
        _attn_sample_kernel,
        grid=(b,),
        in_specs=[row, row, row, cache, cache,
                  _resident((t, lbuf), lambda i: (0, 0)), _resident((t, t), lambda i: (0, 0))],
        out_specs=row,
        out_shape=jax.ShapeDtypeStruct((t, b * w), F32),
        compiler_params=_params(1),
        name="attn_sample",
    )(*(a.reshape(t, b * w) for a in (q, kn, vn)), kc, vc, jnp.asarray(mc), jnp.asarray(mn))
    return out.reshape(m, w)


def _s5_prep_kernel(are_ref, aim_ref, ldt_ref, br_ref, bi_ref, abr_ref, abi_ref, bbr_ref, bbi_ref):
    lam_re = jnp.minimum(are_ref[0], LAMBDA_RE_MAX)
    lam_im = aim_ref[0]
    dt = jnp.exp(ldt_ref[0])
    mag = jnp.exp(lam_re * dt)
    ab_re = mag * jnp.cos(lam_im * dt)
    ab_im = mag * jnp.sin(lam_im * dt)
    den = lam_re * lam_re + lam_im * lam_im
    f_re = ((ab_re - 1.0) * lam_re + ab_im * lam_im) / den
    f_im = (ab_im * lam_re - (ab_re - 1.0) * lam_im) / den
    abr_ref[0] = ab_re
    abi_ref[0] = ab_im
    br = br_ref[0]
    bi = bi_ref[0]
    bbr_ref[0] = (f_re * br - f_im * bi).astype(BF16)
    bbi_ref[0] = (f_re * bi + f_im * br).astype(BF16)


def _s5_prep(a_re, a_im, log_dt, b_re_blk, b_im_blk):
    nl, sw, gn = b_re_blk.shape
    vec = pl.BlockSpec((1, 1, gn), lambda l: (l, 0, 0))
    mat = pl.BlockSpec((1, sw, gn), lambda l: (l, 0, 0))
    return pl.pallas_call(
        _s5_prep_kernel,
        grid=(nl,),
        in_specs=[vec, vec, vec, mat, mat],
        out_specs=[vec, vec, mat, mat],
        out_shape=[jax.ShapeDtypeStruct((nl, 1, gn), F32)] * 2 + [jax.ShapeDtypeStruct((nl, sw, gn), BF16)] * 2,
        compiler_params=_params(1),
        name="s5_prep",
    )(a_re, a_im, log_dt, b_re_blk, b_im_blk)


def _s5_kernel(u_ref, h0r_ref, h0i_ref, abr_ref, abi_ref, bbr_ref, bbi_ref, cr_ref, ci_ref, d_ref,
               gw_ref, gb_ref, y_ref, hr_ref, hi_ref, sr, si):
    nb, gn = hr_ref.shape
    tc = u_ref.shape[0] // nb

    @pl.when(pl.program_id(0) == 0)
    def _():
        hr_ref[...] = h0r_ref[0]
        hi_ref[...] = h0i_ref[0]

    u = u_ref[...]
    ub = u.astype(BF16)
    sr[...] = _dot(ub, bbr_ref[0])
    si[...] = _dot(ub, bbi_ref[0])

    cw = min(gn, SCAN_VREGS * SUBLANES * LANES // nb)
    for c0 in range(0, gn, cw):
        cols = slice(c0, c0 + cw)
        ar = jnp.broadcast_to(abr_ref[0, :, cols], (nb, cw))
        ai = jnp.broadcast_to(abi_ref[0, :, cols], (nb, cw))

        def step(t, h, cols=cols, ar=ar, ai=ai):
            hr, hi = h
            rows = pl.ds(pl.multiple_of(t * nb, nb), nb)
            nr = ar * hr - ai * hi + sr[rows, cols]
            ni = ar * hi + ai * hr + si[rows, cols]
            sr[rows, cols] = nr
            si[rows, cols] = ni
            return nr, ni

        hr, hi = lax.fori_loop(0, tc, step, (hr_ref[:, cols], hi_ref[:, cols]))
        hr_ref[:, cols] = hr
        hi_ref[:, cols] = hi

    y = (_dot(sr[...].astype(BF16), cr_ref[0]) - _dot(si[...].astype(BF16), ci_ref[0])) + d_ref[0] * u
    z = jax.nn.gelu(y)
    y_ref[...] = z * jax.nn.sigmoid(_dot(z.astype(BF16), gw_ref[0]) + gb_ref[0])


def _s5(u, h0r, h0i, abr, abi, bbr, bbi, cr, ci, dskip, gw, gb, layer, h0_layer, tc):
    m, sw = u.shape
    _, nb, gn = h0r.shape
    assert nb % SUBLANES == 0
    rows = tc * nb
    lay = lambda *shape: _resident((1,) + shape, lambda i: (layer,) + (0,) * len(shape))
    h0 = _resident((1, nb, gn), lambda i: (h0_layer, 0, 0))
    state = pl.BlockSpec((nb, gn), lambda i: (0, 0))
    blk = pl.BlockSpec((rows, sw), lambda i: (i, 0))
    return pl.pallas_call(
        _s5_kernel,
        grid=(m // rows,),
        in_specs=[blk, h0, h0, lay(1, gn), lay(1, gn), lay(sw, gn), lay(sw, gn), lay(gn, sw), lay(gn, sw),
                  lay(1, sw), lay(sw, sw), lay(1, sw)],
        out_specs=[blk, state, state],
        out_shape=[jax.ShapeDtypeStruct((m, sw), F32)] + [jax.ShapeDtypeStruct((nb, gn), F32)] * 2,
        scratch_shapes=[pltpu.VMEM((rows, gn), F32)] * 2,
        compiler_params=_params(1),
        name="s5",
    )(u, h0r, h0i, abr, abi, bbr, bbi, cr, ci, dskip, gw, gb)


def _mix_ffn_kernel(x_ref, a_ref, s_ref, ag_ref, sg_ref, wo_ref, n2_ref, wg_ref, wu_ref, wd_ref, fg_ref,
                    y_ref, *, ff_chunk, final):
    aw = a_ref.shape[-1]
    an = _rms(a_ref[...], ag_ref[0]).astype(BF16)
    sn = _rms(s_ref[...], sg_ref[0]).astype(BF16)
    h = x_ref[...] + (_dot(an, wo_ref[0, :aw, :]) + _dot(sn, wo_ref[0, aw:, :]))
    hn = _rms(h, n2_ref[0]).astype(BF16)
    ffn = None
    for c0 in range(0, wg_ref.shape[-1], ff_chunk):
        cols = slice(c0, c0 + ff_chunk)
        act = jax.nn.silu(_dot(hn, wg_ref[0, :, cols])) * _dot(hn, wu_ref[0, :, cols])
        part = _dot(act.astype(BF16), wd_ref[0, cols, :])
        ffn = part if ffn is None else ffn + part
    y = h + ffn
    if final:
        y = _rms(y, fg_ref[...])
    y_ref[...] = y


def _mix_ffn(x, attn, ssm, ag, sg, wo, n2, wg, wu, wd, fg, layer, final):
    m, dm = x.shape
    aw, sw, dff = attn.shape[-1], ssm.shape[-1], wg.shape[-1]
    ff_chunk = dff // 2
    assert ff_chunk % LANES == 0
    tm = min(m, ROW_TILE)
    row = lambda width: pl.BlockSpec((tm, width), lambda i: (i, 0))
    lay = lambda *shape: _resident((1,) + shape, lambda i: (layer,) + (0,) * len(shape))
    return pl.pallas_call(
        functools.partial(_mix_ffn_kernel, ff_chunk=ff_chunk, final=final),
        grid=(m // tm,),
        in_specs=[row(dm), row(aw), row(sw), lay(1, aw), lay(1, sw), lay(aw + sw, dm), lay(1, dm),
                  lay(dm, dff), lay(dm, dff), lay(dff, dm), _resident((1, dm), lambda i: (0, 0))],
        out_specs=row(dm),
        out_shape=jax.ShapeDtypeStruct((m, dm), F32),
        compiler_params=_params(1),
        name="mix_ffn",
    )(x, attn, ssm, ag, sg, wo, n2, wg, wu, wd, fg)


def _block_diag(w, eye):
    nl, g, p, q = w.shape
    return jnp.einsum("lgpq,gh->lgphq", w, eye).reshape(nl, g * p, g * q)


def _time_major(x):
    return jnp.swapaxes(x, 0, 1).reshape((x.shape[0] * x.shape[1],) + x.shape[2:])


def _batch_major(x, nb, *tail):
    return jnp.swapaxes(x.reshape((x.shape[0] // nb, nb) + tail), 0, 1)


def kernel(x_prompt, x_sample, cache_attn_k, cache_attn_v, state_ssm_re, state_ssm_im, norm1_g, w_in, attn_out_g, ssm_a_re, ssm_a_im, ssm_log_dt, ssm_b_re, ssm_b_im, ssm_c_re, ssm_c_im, ssm_d, ssm_glu_w, ssm_glu_b, ssm_out_g, w_out, norm2_g, ffn_w_gate, ffn_w_up, ffn_w_down, final_norm_g):
    bp, seq, dm = x_prompt.shape
    bs, tdec, _ = x_sample.shape
    depth, _, lbuf, n_heads, head_dim = cache_attn_k.shape
    _, n_groups, state_n = ssm_a_re.shape
    gn = n_groups * state_n
    assert head_dim == HEAD_DIM and state_n == STATE_N and lbuf == min(max(w for w, _ in BRANCHES), lbuf)

    vec = lambda a: a.reshape(depth, 1, -1)
    w_in_b, w_out_b = w_in.astype(BF16), w_out.astype(BF16)
    wg_b, wu_b, wd_b = ffn_w_gate.astype(BF16), ffn_w_up.astype(BF16), ffn_w_down.astype(BF16)
    glu_w_b = ssm_glu_w.astype(BF16)
    eye = jnp.eye(n_groups, dtype=F32)
    b_re_blk = _block_diag(jnp.swapaxes(ssm_b_re, 2, 3), eye)
    b_im_blk = _block_diag(jnp.swapaxes(ssm_b_im, 2, 3), eye)
    c_re_blk = _block_diag(jnp.swapaxes(ssm_c_re, 2, 3), eye).astype(BF16)
    c_im_blk = _block_diag(jnp.swapaxes(ssm_c_im, 2, 3), eye).astype(BF16)
    log_dt = jnp.repeat(ssm_log_dt, state_n, axis=-1)
    abr, abi, bbr, bbi = _s5_prep(vec(ssm_a_re), vec(ssm_a_im), vec(log_dt), b_re_blk, b_im_blk)
    n1, n2, ag, sg = vec(norm1_g), vec(norm2_g), vec(attn_out_g), vec(ssm_out_g)
    dsk, gb = vec(ssm_d), vec(ssm_glu_b)
    fg = final_norm_g.reshape(1, dm)
    kc = jnp.transpose(cache_attn_k, (0, 1, 3, 4, 2))
    vc = jnp.transpose(cache_attn_v, (0, 1, 3, 4, 2))
    h0p = jnp.zeros((1, bp, gn), F32)
    h0s_re = state_ssm_re.reshape(depth, bs, gn)
    h0s_im = state_ssm_im.reshape(depth, bs, gn)

    xp = _time_major(x_prompt)
    xs = _time_major(x_sample)
    pk, pv, pr, pim, sk, sv, sr, sim = ([] for _ in range(8))
    for l in range(depth):
        final = l == depth - 1
        s5w = (abr, abi, bbr, bbi, c_re_blk, c_im_blk, dsk, glu_w_b, gb)
        ffw = (ag, sg, w_out_b, n2, wg_b, wu_b, wd_b, fg)

        q, k, v, u = _in_proj(xp, n1, w_in_b, l)
        attn = _attn_prompt(q, k, v, seq)
        ssm, h_re, h_im = _s5(u, h0p, h0p, *s5w, layer=l, h0_layer=0, tc=SCAN_CHUNK)
        xp = _mix_ffn(xp, attn, ssm, *ffw, layer=l, final=final)
        pk.append(_batch_major(k, bp, n_heads, head_dim)[:, -lbuf:])
        pv.append(_batch_major(v, bp, n_heads, head_dim)[:, -lbuf:])
        pr.append(h_re.reshape(bp, n_groups, state_n))
        pim.append(h_im.reshape(bp, n_groups, state_n))

        q, k, v, u = _in_proj(xs, n1, w_in_b, l)
        attn = _attn_sample(q, k, v, kc, vc, l, tdec)
        ssm, h_re, h_im = _s5(u, h0s_re, h0s_im, *s5w, layer=l, h0_layer=l, tc=tdec)
        xs = _mix_ffn(xs, attn, ssm, *ffw, layer=l, final=final)
        sk.append(_batch_major(k, bs, n_heads, head_dim))
        sv.append(_batch_major(v, bs, n_heads, head_dim))
        sr.append(h_re.reshape(bs, n_groups, state_n))
        sim.append(h_im.reshape(bs, n_groups, state_n))

    return (_batch_major(xp, bp, dm), _batch_major(xs, bs, dm),
            jnp.stack(pk), jnp.stack(pv), jnp.stack(pr), jnp.stack(pim),
            jnp.stack(sk), jnp.stack(sv), jnp.stack(sr), jnp.stack(sim))
```

```python
import functools

import numpy as np
import jax
import jax.numpy as jnp
from jax import lax
from jax.experimental import pallas as pl
from jax.experimental.pallas import tpu as pltpu

F32 = jnp.float32
BF16 = jnp.bfloat16

HEAD_DIM = 64
BRANCHES = ((128, 1), (512, 4), (2048, 16))
BRANCH_DIL = tuple(d for _, d in BRANCHES)
BAND = 128
assert all(w // d == BAND for w, d in BRANCHES)
SSM_GROUP_CH = 16
STATE_N = 64
EPS = 1e-6
LAMBDA_RE_MAX = -1e-4

LANES = 128
SUBLANES = 8
VMEM_LIMIT = 56 * 1024 * 1024

ROW_TILE = 512
SCAN_CHUNK = 128
SCAN_VREGS = 8
S5_DIAG = 2


def _resident(shape, index_map):
    return pl.BlockSpec(shape, index_map, pipeline_mode=pl.Buffered(1))


def _params(n_axes):
    return pltpu.CompilerParams(dimension_semantics=("arbitrary",) * n_axes,
                                vmem_limit_bytes=VMEM_LIMIT)


def _rms(x, g):
    return x * lax.rsqrt(jnp.mean(x * x, axis=-1, keepdims=True) + EPS) * g


def _nt(a, b):
    return lax.dot_general(a, b, (((1,), (1,)), ((), ())), preferred_element_type=F32)


def _dot(a, b):
    return jnp.dot(a, b, preferred_element_type=F32)


def _in_proj_kernel(x_ref, g_ref, w_ref, *refs, transposed_kv):
    q_ref, k_ref, v_ref, u_ref = refs[-6:-2] if transposed_kv else refs[-4:]
    xn = _rms(x_ref[...], g_ref[0])
    z = _dot(xn.astype(BF16), w_ref[0])
    aw = q_ref.shape[-1]
    q_ref[...] = z[:, :aw]
    k_ref[...] = z[:, aw:2 * aw]
    v_ref[...] = z[:, 2 * aw:3 * aw]
    for j in range(u_ref.shape[0]):
        u_ref[j] = z[:, 3 * aw + j * LANES:3 * aw + (j + 1) * LANES]
    if transposed_kv:
        kt_ref, vt_ref = refs[-2:]
        kt_ref[0, 0] = z[:, aw:2 * aw].T
        vt_ref[0, 0] = z[:, 2 * aw:3 * aw].T


def _in_proj(x, g, w, layer, kv_buffers=None, seq=None):
    m, dm = x.shape
    n = w.shape[-1]
    sw = g.shape[-1] // 2
    aw = (n - sw) // 3
    tm = min(m, ROW_TILE)
    row = lambda width: pl.BlockSpec((tm, width), lambda i: (i, 0))
    in_specs = [row(dm),
                _resident((1, 1, dm), lambda i: (layer, 0, 0)),
                _resident((1, dm, n), lambda i: (layer, 0, 0))]
    out_specs = [row(aw), row(aw), row(aw), pl.BlockSpec((sw // LANES, tm, LANES), lambda i: (0, i, 0))]
    out_shape = [jax.ShapeDtypeStruct((m, aw), F32)] * 3 + [jax.ShapeDtypeStruct((sw // LANES, m, LANES), F32)]
    args = [x, g, w]
    aliases = {}
    if kv_buffers is not None:
        buf_shape, kt, vt = kv_buffers
        tiles = seq // tm
        kv_spec = pl.BlockSpec((1, 1, aw, tm), lambda i: (layer, i // tiles, 0, i % tiles))
        out_specs += [kv_spec, kv_spec]
        out_shape += [jax.ShapeDtypeStruct(buf_shape, F32)] * 2
        if kt is not None:
            in_specs += [pl.BlockSpec(memory_space=pl.ANY)] * 2
            args += [kt, vt]
            aliases = {3: 4, 4: 5}
    return pl.pallas_call(
        functools.partial(_in_proj_kernel, transposed_kv=kv_buffers is not None),
        grid=(m // tm,),
        in_specs=in_specs,
        out_specs=out_specs,
        out_shape=out_shape,
        input_output_aliases=aliases,
        compiler_params=_params(1),
        name="in_proj",
    )(*args)


def _attn_prompt_kernel(q_ref, k_ref, v_ref, o_ref, qlo, qhi, kb, vlo, vhi, a_s, l_s, m_s, s_ring, p_ring, *, seq):
    nblk = seq // BAND
    lane = lax.broadcasted_iota(jnp.int32, (BAND, LANES), 1)
    lo = lane < HEAD_DIM
    row = lax.broadcasted_iota(jnp.int32, (BAND, BAND), 0)
    col = lax.broadcasted_iota(jnp.int32, (BAND, BAND), 1)
    causal = col <= row
    anti = col >= row
    zero = jnp.zeros((BAND, LANES), F32)

    for bi, d in enumerate(BRANCH_DIL):
        nb = nblk // d
        pad = jnp.zeros((BAND, LANES), BF16)
        kb[bi, :BAND, :] = pad
        vlo[bi, :BAND, :] = pad
        vhi[bi, :BAND, :] = pad

        def natural(idx, d=d, nb=nb):
            r = idx // nb
            c = idx % nb
            st = r + d * BAND * c
            return (pl.ds(st, BAND, stride=d) if d > 1 else pl.ds(pl.multiple_of(st, BAND), BAND)), c

        def packed(idx):
            return pl.ds(pl.multiple_of(idx * BAND, BAND), BAND)

        def stage(idx, carry, bi=bi, natural=natural):
            src, _ = natural(idx)
            qv = q_ref[0, src, :] * (HEAD_DIM ** -0.5)
            qlo[bi, packed(idx), :] = jnp.where(lo, qv, zero).astype(BF16)
            qhi[bi, packed(idx), :] = jnp.where(lo, zero, qv).astype(BF16)
            kb[bi, packed(idx + 1), :] = k_ref[0, src, :].astype(BF16)
            vv = v_ref[0, src, :]
            vlo[bi, packed(idx + 1), :] = jnp.where(lo, vv, zero).astype(BF16)
            vhi[bi, packed(idx + 1), :] = jnp.where(lo, zero, vv).astype(BF16)
            return carry

        lax.fori_loop(0, nblk, stage, 0)

        span = 2 * BAND if nb > 1 else BAND

        def slab(idx, nb=nb, span=span):
            return pl.ds(pl.multiple_of((idx if nb > 1 else idx + 1) * BAND, BAND), span)

        def scores(idx, slot, bi=bi, slab=slab, span=span):
            keys = kb[bi, slab(idx), :]
            for h, qr in enumerate((qlo, qhi)):
                s_ring[slot, h, :, :span] = _nt(qr[bi, packed(idx), :], keys)

        def softmax(idx, slot, bi=bi, nb=nb, natural=natural, span=span):
            dst, c = natural(idx)
            valid = causal
            if nb > 1:
                valid = jnp.concatenate([jnp.logical_and(anti, c > 0), causal], axis=1)
            ms = []
            for h in range(2):
                s = jnp.where(valid, s_ring[slot, h, :, :span], -jnp.inf)
                tile_max = jnp.maximum(s[:, :BAND], s[:, BAND:]) if nb > 1 else s
                m = jnp.max(tile_max, axis=-1, keepdims=True)
                p_ring[slot, h, :, :span] = jnp.exp(s - m).astype(BF16)
                ms.append(m)
            m_s[bi, dst, :] = jnp.where(lo, ms[0], ms[1])

        def values(idx, slot, bi=bi, natural=natural, slab=slab, span=span):
            dst, _ = natural(idx)
            ones = jnp.ones((span, LANES), BF16)
            accs = [_dot(p_ring[slot, h, :, :span], jnp.concatenate([vr[bi, slab(idx), :], ones], axis=1))
                    for h, vr in enumerate((vlo, vhi))]
            a_s[bi, dst, :] = accs[0][:, :LANES] + accs[1][:, :LANES]
            l_s[bi, dst, :] = jnp.where(lo, accs[0][:, LANES:], accs[1][:, LANES:])

        scores(0, 0)
        scores(1, 1)
        softmax(0, 0)

        def trip(i, carry, scores=scores, softmax=softmax, values=values):
            j = 2 * i
            values(j - 2, 0)
            softmax(j - 1, 1)
            scores(j, 0)
            values(j - 1, 1)
            softmax(j, 0)
            scores(j + 1, 1)
            return carry

        lax.fori_loop(1, nblk // 2, trip, 0)
        values(nblk - 2, 0)
        softmax(nblk - 1, 1)
        values(nblk - 1, 1)

    def combine(i, carry):
        sl = pl.ds(pl.multiple_of(i * BAND, BAND), BAND)
        ms = [m_s[bi, sl, :] for bi in range(len(BRANCH_DIL))]
        mm = functools.reduce(jnp.maximum, ms)
        fs = [jnp.exp(m - mm) for m in ms]
        num = sum(f * a_s[bi, sl, :] for bi, f in enumerate(fs))
        den = sum(f * l_s[bi, sl, :] for bi, f in enumerate(fs))
        o_ref[0, sl, :] = num / den
        return carry

    lax.fori_loop(0, nblk, combine, 0, unroll=2)


def _attn_prompt(q, k, v, seq):
    m, w = q.shape
    b = m // seq
    assert seq % (BAND * max(BRANCH_DIL)) == 0 and w % LANES == 0
    spec = pl.BlockSpec((1, seq, LANES), lambda i, j: (i, 0, j))
    nbr = len(BRANCH_DIL)
    out = pl.pallas_call(
        functools.partial(_attn_prompt_kernel, seq=seq),
        grid=(b, w // LANES),
        in_specs=[spec, spec, spec],
        out_specs=spec,
        out_shape=jax.ShapeDtypeStruct((b, seq, w), F32),
        scratch_shapes=([pltpu.VMEM((nbr, seq, LANES), BF16)] * 2 + [pltpu.VMEM((nbr, seq + BAND, LANES), BF16)] * 3
                        + [pltpu.VMEM((nbr, seq, LANES), F32)] * 3
                        + [pltpu.VMEM((2, 2, BAND, 2 * BAND), F32), pltpu.VMEM((2, 2, BAND, 2 * BAND), BF16)]),
        compiler_params=_params(2),
        name="attn_prompt",
    )(*(t.reshape(b, seq, w) for t in (q, k, v)))
    return out.reshape(m, w)


def _sample_multiplicity(lbuf, t):
    dist_c = lbuf + np.arange(t)[:, None] - np.arange(lbuf)[None, :]
    dist_n = np.arange(t)[:, None] - np.arange(t)[None, :]
    mc = np.zeros((t, lbuf), np.float32)
    mn = np.zeros((t, t), np.float32)
    for w, d in BRANCHES:
        mc += (dist_c % d == 0) & (dist_c <= w)
        mn += (dist_n >= 0) & (dist_n % d == 0) & (dist_n <= w)
    return mc, mn


def _attn_sample_kernel(q_ref, kn_ref, vn_ref, kc_ref, vc_ref, mc_ref, mn_ref, o_ref):
    n_heads = kc_ref.shape[2]
    mc = mc_ref[...]
    mn = mn_ref[...]
    outs = []
    for h in range(n_heads):
        hs = slice(h * HEAD_DIM, (h + 1) * HEAD_DIM)
        qh = (q_ref[:, hs] * (HEAD_DIM ** -0.5)).astype(BF16)
        s_c = jnp.where(mc > 0, _dot(qh, kc_ref[0, 0, h].astype(BF16)), -jnp.inf)
        s_n = jnp.where(mn > 0, _nt(qh, kn_ref[:, hs].astype(BF16)), -jnp.inf)
        m = jnp.maximum(jnp.max(s_c, axis=-1, keepdims=True), jnp.max(s_n, axis=-1, keepdims=True))
        p_c = mc * jnp.exp(s_c - m)
        p_n = mn * jnp.exp(s_n - m)
        den = jnp.sum(p_c, axis=-1, keepdims=True) + jnp.sum(p_n, axis=-1, keepdims=True)
        num = (_nt(p_c.astype(BF16), vc_ref[0, 0, h].astype(BF16))
               + _dot(p_n.astype(BF16), vn_ref[:, hs].astype(BF16)))
        outs.append(num / den)
    o_ref[...] = jnp.concatenate(outs, axis=-1)


def _attn_sample(q, kn, vn, kc, vc, layer, t):
    m, w = q.shape
    _, b, h, e, lbuf = kc.shape
    for win, d in BRANCHES:
        assert lbuf - d * (win // d) >= 0
    mc, mn = _sample_multiplicity(lbuf, t)
    row = pl.BlockSpec((t, w), lambda i: (i, 0))
    cache = pl.BlockSpec((1, 1, h, e, lbuf), lambda i: (layer, i, 0, 0, 0))
    return pl.pallas_call(
        _attn_sample_kernel,
        grid=(b,),
        in_specs=[row, row, row, cache, cache,
                  _resident((t, lbuf), lambda i: (0, 0)), _resident((t, t), lambda i: (0, 0))],
        out_specs=row,
        out_shape=jax.ShapeDtypeStruct((m, w), F32),
        compiler_params=_params(1),
        name="attn_sample",
    )(q, kn, vn, kc, vc, jnp.asarray(mc), jnp.asarray(mn))


def _s5_prep_kernel(are_ref, aim_ref, ldt_ref, br_ref, bi_ref, abr_ref, abi_ref, bbr_ref, bbi_ref):
    lam_re = jnp.minimum(are_ref[0], LAMBDA_RE_MAX)
    lam_im = aim_ref[0]
    dt = jnp.exp(ldt_ref[0])
    mag = jnp.exp(lam_re * dt)
    ab_re = mag * jnp.cos(lam_im * dt)
    ab_im = mag * jnp.sin(lam_im * dt)
    den = lam_re * lam_re + lam_im * lam_im
    f_re = ((ab_re - 1.0) * lam_re + ab_im * lam_im) / den
    f_im = (ab_im * lam_re - (ab_re - 1.0) * lam_im) / den
    abr_ref[0] = ab_re
    abi_ref[0] = ab_im
    br = br_ref[0]
    bi = bi_ref[0]
    bbr_ref[0] = (f_re * br - f_im * bi).astype(BF16)
    bbi_ref[0] = (f_re * bi + f_im * br).astype(BF16)


def _s5_prep(a_re, a_im, log_dt, b_re_blk, b_im_blk):
    nl, sw, gn = b_re_blk.shape
    vec = pl.BlockSpec((1, 1, gn), lambda l: (l, 0, 0))
    mat = pl.BlockSpec((1, sw, gn), lambda l: (l, 0, 0))
    return pl.pallas_call(
        _s5_prep_kernel,
        grid=(nl,),
        in_specs=[vec, vec, vec, mat, mat],
        out_specs=[vec, vec, mat, mat],
        out_shape=[jax.ShapeDtypeStruct((nl, 1, gn), F32)] * 2 + [jax.ShapeDtypeStruct((nl, sw, gn), BF16)] * 2,
        compiler_params=_params(1),
        name="s5_prep",
    )(a_re, a_im, log_dt, b_re_blk, b_im_blk)


def _s5_kernel(u_ref, h0r_ref, h0i_ref, abr_ref, abi_ref, bbr_ref, bbi_ref, cr_ref, ci_ref, d_ref,
               gw_ref, gb_ref, y_ref, hr_ref, hi_ref, ut, sr, si):
    n_chunk, nb, tc, _ = u_ref.shape
    gn = hr_ref.shape[-1]

    @pl.when(pl.program_id(0) == 0)
    def _():
        hr_ref[...] = h0r_ref[0]
        hi_ref[...] = h0i_ref[0]

    for j in range(n_chunk):
        for b in range(nb):
            ut[j, pl.ds(b, tc, stride=nb), :] = u_ref[j, b]
    u = jnp.concatenate([ut[j] for j in range(n_chunk)], axis=-1)
    ub = u.astype(BF16)
    sw = n_chunk * LANES
    dc, dn = sw // S5_DIAG, gn // S5_DIAG
    for i in range(S5_DIAG):
        ch, st = slice(i * dc, (i + 1) * dc), slice(i * dn, (i + 1) * dn)
        sr[:, st] = _dot(ub[:, ch], bbr_ref[0, ch, st])
        si[:, st] = _dot(ub[:, ch], bbi_ref[0, ch, st])

    cw = min(gn, SCAN_VREGS * SUBLANES * LANES // nb)
    for c0 in range(0, gn, cw):
        cols = slice(c0, c0 + cw)
        ar = jnp.broadcast_to(abr_ref[0, :, cols], (nb, cw))
        ai = jnp.broadcast_to(abi_ref[0, :, cols], (nb, cw))

        def step(t, h, cols=cols, ar=ar, ai=ai):
            hr, hi = h
            rows = pl.ds(pl.multiple_of(t * nb, nb), nb)
            nr = ar * hr - ai * hi + sr[rows, cols]
            ni = ar * hi + ai * hr + si[rows, cols]
            sr[rows, cols] = nr
            si[rows, cols] = ni
            return nr, ni

        hr, hi = lax.fori_loop(0, tc, step, (hr_ref[:, cols], hi_ref[:, cols]))
        hr_ref[:, cols] = hr
        hi_ref[:, cols] = hi

    ch_parts = []
    for i in range(S5_DIAG):
        ch, st = slice(i * dc, (i + 1) * dc), slice(i * dn, (i + 1) * dn)
        ch_parts.append(_dot(sr[:, st].astype(BF16), cr_ref[0, st, ch]) - _dot(si[:, st].astype(BF16), ci_ref[0, st, ch]))
    y = jnp.concatenate(ch_parts, axis=-1) + d_ref[0] * u
    z = jax.nn.gelu(y)
    out = z * jax.nn.sigmoid(_dot(z.astype(BF16), gw_ref[0]) + gb_ref[0])
    for j in range(n_chunk):
        ut[j] = out[:, j * LANES:(j + 1) * LANES]
    for j in range(n_chunk):
        for b in range(nb):
            y_ref[b, :, j * LANES:(j + 1) * LANES] = ut[j, pl.ds(b, tc, stride=nb), :]


def _s5(u, h0r, h0i, abr, abi, bbr, bbi, cr, ci, dskip, gw, gb, layer, h0_layer, nb, tc):
    n_chunk, m, _ = u.shape
    sw = n_chunk * LANES
    t = m // nb
    gn = h0r.shape[-1]
    assert nb % SUBLANES == 0 and tc % SUBLANES == 0
    assert (sw // SSM_GROUP_CH) % S5_DIAG == 0 and (sw // S5_DIAG) % LANES == 0 and (gn // S5_DIAG) % LANES == 0
    lay = lambda *shape: _resident((1,) + shape, lambda i: (layer,) + (0,) * len(shape))
    h0 = _resident((1, nb, gn), lambda i: (h0_layer, 0, 0))
    state = pl.BlockSpec((nb, gn), lambda i: (0, 0))
    y, h_re, h_im = pl.pallas_call(
        _s5_kernel,
        grid=(t // tc,),
        in_specs=[pl.BlockSpec((n_chunk, nb, tc, LANES), lambda i: (0, 0, i, 0)), h0, h0,
                  lay(1, gn), lay(1, gn), lay(sw, gn), lay(sw, gn), lay(gn, sw), lay(gn, sw),
                  lay(1, sw), lay(sw, sw), lay(1, sw)],
        out_specs=[pl.BlockSpec((nb, tc, sw), lambda i: (0, i, 0)), state, state],
        out_shape=[jax.ShapeDtypeStruct((nb, t, sw), F32)] + [jax.ShapeDtypeStruct((nb, gn), F32)] * 2,
        scratch_shapes=[pltpu.VMEM((n_chunk, tc * nb, LANES), F32)] + [pltpu.VMEM((tc * nb, gn), F32)] * 2,
        compiler_params=_params(1),
        name="s5",
    )(u.reshape(n_chunk, nb, t, LANES), h0r, h0i, abr, abi, bbr, bbi, cr, ci, dskip, gw, gb)
    return y.reshape(m, sw), h_re, h_im


def _mix_ffn_kernel(x_ref, a_ref, s_ref, ag_ref, sg_ref, wo_ref, n2_ref, wg_ref, wu_ref, wd_ref, fg_ref,
                    y_ref, *, ff_chunk, final):
    aw = a_ref.shape[-1]
    an = _rms(a_ref[...], ag_ref[0]).astype(BF16)
    sn = _rms(s_ref[...], sg_ref[0]).astype(BF16)
    h = x_ref[...] + (_dot(an, wo_ref[0, :aw, :]) + _dot(sn, wo_ref[0, aw:, :]))
    hn = _rms(h, n2_ref[0]).astype(BF16)
    ffn = None
    for c0 in range(0, wg_ref.shape[-1], ff_chunk):
        cols = slice(c0, c0 + ff_chunk)
        act = jax.nn.silu(_dot(hn, wg_ref[0, :, cols])) * _dot(hn, wu_ref[0, :, cols])
        part = _dot(act.astype(BF16), wd_ref[0, cols, :])
        ffn = part if ffn is None else ffn + part
    y = h + ffn
    if final:
        y = _rms(y, fg_ref[...])
    y_ref[...] = y


def _mix_ffn(x, attn, ssm, ag, sg, wo, n2, wg, wu, wd, fg, layer, final):
    m, dm = x.shape
    aw, sw, dff = attn.shape[-1], ssm.shape[-1], wg.shape[-1]
    ff_chunk = dff // 2
    assert ff_chunk % LANES == 0
    tm = min(m, ROW_TILE)
    row = lambda width: pl.BlockSpec((tm, width), lambda i: (i, 0))
    lay = lambda *shape: _resident((1,) + shape, lambda i: (layer,) + (0,) * len(shape))
    return pl.pallas_call(
        functools.partial(_mix_ffn_kernel, ff_chunk=ff_chunk, final=final),
        grid=(m // tm,),
        in_specs=[row(dm), row(aw), row(sw), lay(1, aw), lay(1, sw), lay(aw + sw, dm), lay(1, dm),
                  lay(dm, dff), lay(dm, dff), lay(dff, dm), _resident((1, dm), lambda i: (0, 0))],
        out_specs=row(dm),
        out_shape=jax.ShapeDtypeStruct((m, dm), F32),
        compiler_params=_params(1),
        name="mix_ffn",
    )(x, attn, ssm, ag, sg, wo, n2, wg, wu, wd, fg)


def _block_diag(w, eye):
    nl, g, p, q = w.shape
    return jnp.einsum("lgpq,gh->lgphq", w, eye).reshape(nl, g * p, g * q)


def kernel(x_prompt, x_sample, cache_attn_k, cache_attn_v, state_ssm_re, state_ssm_im, norm1_g, w_in, attn_out_g, ssm_a_re, ssm_a_im, ssm_log_dt, ssm_b_re, ssm_b_im, ssm_c_re, ssm_c_im, ssm_d, ssm_glu_w, ssm_glu_b, ssm_out_g, w_out, norm2_g, ffn_w_gate, ffn_w_up, ffn_w_down, final_norm_g):
    bp, seq, dm = x_prompt.shape
    bs, tdec, _ = x_sample.shape
    depth, _, lbuf, n_heads, head_dim = cache_attn_k.shape
    _, n_groups, state_n = ssm_a_re.shape
    gn = n_groups * state_n
    aw = n_heads * head_dim
    assert head_dim == HEAD_DIM and state_n == STATE_N and lbuf == min(max(w for w, _ in BRANCHES), lbuf)

    vec = lambda a: a.reshape(depth, 1, -1)
    w_in_b, w_out_b = w_in.astype(BF16), w_out.astype(BF16)
    wg_b, wu_b, wd_b = ffn_w_gate.astype(BF16), ffn_w_up.astype(BF16), ffn_w_down.astype(BF16)
    glu_w_b = ssm_glu_w.astype(BF16)
    eye = jnp.eye(n_groups, dtype=F32)
    b_re_blk = _block_diag(jnp.swapaxes(ssm_b_re, 2, 3), eye)
    b_im_blk = _block_diag(jnp.swapaxes(ssm_b_im, 2, 3), eye)
    c_re_blk = _block_diag(jnp.swapaxes(ssm_c_re, 2, 3), eye).astype(BF16)
    c_im_blk = _block_diag(jnp.swapaxes(ssm_c_im, 2, 3), eye).astype(BF16)
    log_dt = jnp.repeat(ssm_log_dt, state_n, axis=-1)
    abr, abi, bbr, bbi = _s5_prep(vec(ssm_a_re), vec(ssm_a_im), vec(log_dt), b_re_blk, b_im_blk)
    n1, n2, ag, sg = vec(norm1_g), vec(norm2_g), vec(attn_out_g), vec(ssm_out_g)
    dsk, gb = vec(ssm_d), vec(ssm_glu_b)
    fg = final_norm_g.reshape(1, dm)
    kc = jnp.transpose(cache_attn_k, (0, 1, 3, 4, 2))
    vc = jnp.transpose(cache_attn_v, (0, 1, 3, 4, 2))
    h0p = jnp.zeros((1, bp, gn), F32)
    h0s_re = state_ssm_re.reshape(depth, bs, gn)
    h0s_im = state_ssm_im.reshape(depth, bs, gn)

    xp = x_prompt.reshape(bp * seq, dm)
    xs = x_sample.reshape(bs * tdec, dm)
    kt = vt = None
    pr, pim, sk, sv, sr, sim = ([] for _ in range(6))
    for l in range(depth):
        final = l == depth - 1
        s5w = (abr, abi, bbr, bbi, c_re_blk, c_im_blk, dsk, glu_w_b, gb)
        ffw = (ag, sg, w_out_b, n2, wg_b, wu_b, wd_b, fg)

        q, k, v, u, kt, vt = _in_proj(xp, n1, w_in_b, l, kv_buffers=((depth, bp, aw, seq), kt, vt), seq=seq)
        attn = _attn_prompt(q, k, v, seq)
        ssm, h_re, h_im = _s5(u, h0p, h0p, *s5w, layer=l, h0_layer=0, nb=bp, tc=SCAN_CHUNK)
        xp = _mix_ffn(xp, attn, ssm, *ffw, layer=l, final=final)
        pr.append(h_re.reshape(bp, n_groups, state_n))
        pim.append(h_im.reshape(bp, n_groups, state_n))

        q, k, v, u = _in_proj(xs, n1, w_in_b, l)
        attn = _attn_sample(q, k, v, kc, vc, l, tdec)
        ssm, h_re, h_im = _s5(u, h0s_re, h0s_im, *s5w, layer=l, h0_layer=l, nb=bs, tc=tdec)
        xs = _mix_ffn(xs, attn, ssm, *ffw, layer=l, final=final)
        sk.append(k.reshape(bs, tdec, n_heads, head_dim))
        sv.append(v.reshape(bs, tdec, n_heads, head_dim))
        sr.append(h_re.reshape(bs, n_groups, state_n))
        sim.append(h_im.reshape(bs, n_groups, state_n))

    window = lambda t: jnp.transpose(t.reshape(depth, bp, n_heads, head_dim, seq), (0, 1, 4, 2, 3))[:, :, -lbuf:]
    return (xp.reshape(bp, seq, dm), xs.reshape(bs, tdec, dm),
            window(kt), window(vt), jnp.stack(pr), jnp.stack(pim),
            jnp.stack(sk), jnp.stack(sv), jnp.stack(sr), jnp.stack(sim))
```

```python
import functools

import numpy as np
import jax
import jax.numpy as jnp
from jax import lax
from jax.experimental import pallas as pl
from jax.experimental.pallas import tpu as pltpu

F32 = jnp.float32
BF16 = jnp.bfloat16

HEAD_DIM = 64
BRANCHES = ((128, 1), (512, 4), (2048, 16))
BRANCH_DIL = tuple(d for _, d in BRANCHES)
BAND = 128
assert all(w // d == BAND for w, d in BRANCHES)
SSM_GROUP_CH = 16
STATE_N = 64
EPS = 1e-6
LAMBDA_RE_MAX = -1e-4

LANES = 128
SUBLANES = 8
VMEM_LIMIT = 56 * 1024 * 1024

ROW_TILE = 512
SCAN_CHUNK = 128
SCAN_VREGS = 8
S5_DIAG = 2


def _resident(shape, index_map):
    return pl.BlockSpec(shape, index_map, pipeline_mode=pl.Buffered(1))


def _params(n_axes):
    return pltpu.CompilerParams(dimension_semantics=("arbitrary",) * n_axes,
                                vmem_limit_bytes=VMEM_LIMIT)


def _rms(x, g):
    return x * lax.rsqrt(jnp.mean(x * x, axis=-1, keepdims=True) + EPS) * g


def _nt(a, b):
    return lax.dot_general(a, b, (((1,), (1,)), ((), ())), preferred_element_type=F32)


def _dot(a, b):
    return jnp.dot(a, b, preferred_element_type=F32)


def _in_proj_kernel(x_ref, g_ref, w_ref, *refs, transposed_kv):
    q_ref, k_ref, v_ref, u_ref = refs[-6:-2] if transposed_kv else refs[-4:]
    xn = _rms(x_ref[...], g_ref[0])
    z = _dot(xn.astype(BF16), w_ref[0])
    aw = q_ref.shape[-1]
    q_ref[...] = z[:, :aw]
    k_ref[...] = z[:, aw:2 * aw]
    v_ref[...] = z[:, 2 * aw:3 * aw]
    for j in range(u_ref.shape[0]):
        u_ref[j] = z[:, 3 * aw + j * LANES:3 * aw + (j + 1) * LANES]
    if transposed_kv:
        kt_ref, vt_ref = refs[-2:]
        kt_ref[0, 0] = z[:, aw:2 * aw].T
        vt_ref[0, 0] = z[:, 2 * aw:3 * aw].T


def _in_proj(x, g, w, layer, kv_buffers=None, seq=None):
    m, dm = x.shape
    n = w.shape[-1]
    sw = g.shape[-1] // 2
    aw = (n - sw) // 3
    tm = min(m, ROW_TILE)
    row = lambda width: pl.BlockSpec((tm, width), lambda i: (i, 0))
    in_specs = [row(dm),
                _resident((1, 1, dm), lambda i: (layer, 0, 0)),
                _resident((1, dm, n), lambda i: (layer, 0, 0))]
    out_specs = [row(aw), row(aw), row(aw), pl.BlockSpec((sw // LANES, tm, LANES), lambda i: (0, i, 0))]
    out_shape = [jax.ShapeDtypeStruct((m, aw), F32)] * 3 + [jax.ShapeDtypeStruct((sw // LANES, m, LANES), F32)]
    args = [x, g, w]
    aliases = {}
    if kv_buffers is not None:
        buf_shape, kt, vt = kv_buffers
        tiles = seq // tm
        kv_spec = pl.BlockSpec((1, 1, aw, tm), lambda i: (layer, i // tiles, 0, i % tiles))
        out_specs += [kv_spec, kv_spec]
        out_shape += [jax.ShapeDtypeStruct(buf_shape, F32)] * 2
        if kt is not None:
            in_specs += [pl.BlockSpec(memory_space=pl.ANY)] * 2
            args += [kt, vt]
            aliases = {3: 4, 4: 5}
    return pl.pallas_call(
        functools.partial(_in_proj_kernel, transposed_kv=kv_buffers is not None),
        grid=(m // tm,),
        in_specs=in_specs,
        out_specs=out_specs,
        out_shape=out_shape,
        input_output_aliases=aliases,
        compiler_params=_params(1),
        name="in_proj",
    )(*args)


def _attn_prompt_kernel(q_ref, k_ref, v_ref, o_ref, qlo, qhi, kb, vlo, vhi, a_s, l_s, m_s, s_ring, p_ring, *, seq):
    nblk = seq // BAND
    lane = lax.broadcasted_iota(jnp.int32, (BAND, LANES), 1)
    lo = lane < HEAD_DIM
    row = lax.broadcasted_iota(jnp.int32, (BAND, BAND), 0)
    col = lax.broadcasted_iota(jnp.int32, (BAND, BAND), 1)
    causal = col <= row
    band2 = jnp.concatenate([col >= row, causal], axis=1)
    zero = jnp.zeros((BAND, LANES), F32)

    def natural(d, idx):
        nb = nblk // d
        st = idx // nb + d * BAND * (idx % nb)
        return pl.ds(st, BAND, stride=d) if d > 1 else pl.ds(pl.multiple_of(st, BAND), BAND)

    def packed(idx):
        return pl.ds(pl.multiple_of(idx * BAND, BAND), BAND)

    for bi, d in enumerate(BRANCH_DIL):
        def stage(idx, carry, bi=bi, d=d):
            src, dst = natural(d, idx), packed(idx)
            qv = q_ref[0, src, :] * (HEAD_DIM ** -0.5)
            qlo[bi, dst, :] = jnp.where(lo, qv, zero).astype(BF16)
            qhi[bi, dst, :] = jnp.where(lo, zero, qv).astype(BF16)
            kb[bi, dst, :] = k_ref[0, src, :].astype(BF16)
            vv = v_ref[0, src, :]
            vlo[bi, dst, :] = jnp.where(lo, vv, zero).astype(BF16)
            vhi[bi, dst, :] = jnp.where(lo, zero, vv).astype(BF16)
            return carry

        lax.fori_loop(0, nblk, stage, 0, unroll=2)

    def slab(d, idx):
        has_prev = idx % (nblk // d) > 0
        return (pl.ds((idx - 1) * BAND, 2 * BAND) if has_prev else pl.ds(idx * BAND, BAND)), has_prev

    def scores(bi, d, idx, slot):
        rows, _ = slab(d, idx)
        keys = kb[bi, rows, :]
        for h, qr in enumerate((qlo, qhi)):
            s_ring[slot, h, :, :keys.shape[0]] = _nt(qr[bi, pl.ds(idx * BAND, BAND), :], keys)

    def softmax(bi, d, idx, slot):
        _, has_prev = slab(d, idx)
        ms = []
        for h in range(2):
            if has_prev:
                s = jnp.where(band2, s_ring[slot, h], -jnp.inf)
                m = jnp.max(jnp.maximum(s[:, :BAND], s[:, BAND:]), axis=-1, keepdims=True)
                p_ring[slot, h] = jnp.exp(s - m).astype(BF16)
            else:
                s = jnp.where(causal, s_ring[slot, h, :, :BAND], -jnp.inf)
                m = jnp.max(s, axis=-1, keepdims=True)
                p_ring[slot, h, :, :BAND] = jnp.exp(s - m).astype(BF16)
            ms.append(m)
        m_s[bi, natural(d, idx), :] = jnp.where(lo, ms[0], ms[1])

    def values(bi, d, idx, slot):
        rows, has_prev = slab(d, idx)
        span = 2 * BAND if has_prev else BAND
        ones = jnp.ones((span, LANES), BF16)
        accs = [_dot(p_ring[slot, h, :, :span], jnp.concatenate([vr[bi, rows, :], ones], axis=1))
                for h, vr in enumerate((vlo, vhi))]
        dst = natural(d, idx)
        a_s[bi, dst, :] = accs[0][:, :LANES] + accs[1][:, :LANES]
        l_s[bi, dst, :] = jnp.where(lo, accs[0][:, LANES:], accs[1][:, LANES:])

    blocks = [(bi, d, idx) for bi, d in enumerate(BRANCH_DIL) for idx in range(nblk)]
    for t in range(len(blocks) + 2):
        if t >= 2:
            values(*blocks[t - 2], t % 2)
        if 1 <= t <= len(blocks):
            softmax(*blocks[t - 1], (t - 1) % 2)
        if t < len(blocks):
            scores(*blocks[t], t % 2)

    def combine(i, carry):
        sl = pl.ds(pl.multiple_of(i * BAND, BAND), BAND)
        ms = [m_s[bi, sl, :] for bi in range(len(BRANCH_DIL))]
        mm = functools.reduce(jnp.maximum, ms)
        fs = [jnp.exp(m - mm) for m in ms]
        num = sum(f * a_s[bi, sl, :] for bi, f in enumerate(fs))
        den = sum(f * l_s[bi, sl, :] for bi, f in enumerate(fs))
        o_ref[0, sl, :] = num / den
        return carry

    lax.fori_loop(0, nblk, combine, 0, unroll=2)


def _attn_prompt(q, k, v, seq):
    m, w = q.shape
    b = m // seq
    assert seq % (BAND * max(BRANCH_DIL)) == 0 and w % LANES == 0
    spec = pl.BlockSpec((1, seq, LANES), lambda i, j: (i, 0, j))
    nbr = len(BRANCH_DIL)
    out = pl.pallas_call(
        functools.partial(_attn_prompt_kernel, seq=seq),
        grid=(b, w // LANES),
        in_specs=[spec, spec, spec],
        out_specs=spec,
        out_shape=jax.ShapeDtypeStruct((b, seq, w), F32),
        scratch_shapes=([pltpu.VMEM((nbr, seq, LANES), BF16)] * 5 + [pltpu.VMEM((nbr, seq, LANES), F32)] * 3
                        + [pltpu.VMEM((2, 2, BAND, 2 * BAND), F32), pltpu.VMEM((2, 2, BAND, 2 * BAND), BF16)]),
        compiler_params=_params(2),
        name="attn_prompt",
    )(*(t.reshape(b, seq, w) for t in (q, k, v)))
    return out.reshape(m, w)


def _sample_multiplicity(lbuf, t):
    dist_c = lbuf + np.arange(t)[:, None] - np.arange(lbuf)[None, :]
    dist_n = np.arange(t)[:, None] - np.arange(t)[None, :]
    mc = np.zeros((t, lbuf), np.float32)
    mn = np.zeros((t, t), np.float32)
    for w, d in BRANCHES:
        mc += (dist_c % d == 0) & (dist_c <= w)
        mn += (dist_n >= 0) & (dist_n % d == 0) & (dist_n <= w)
    return mc, mn


def _attn_sample_kernel(q_ref, kn_ref, vn_ref, kc_ref, vc_ref, mc_ref, mn_ref, o_ref):
    n_heads = kc_ref.shape[2]
    mc = mc_ref[...]
    mn = mn_ref[...]
    outs = []
    for h in range(n_heads):
        hs = slice(h * HEAD_DIM, (h + 1) * HEAD_DIM)
        qh = (q_ref[:, hs] * (HEAD_DIM ** -0.5)).astype(BF16)
        s_c = jnp.where(mc > 0, _dot(qh, kc_ref[0, 0, h].astype(BF16)), -jnp.inf)
        s_n = jnp.where(mn > 0, _nt(qh, kn_ref[:, hs].astype(BF16)), -jnp.inf)
        m = jnp.maximum(jnp.max(s_c, axis=-1, keepdims=True), jnp.max(s_n, axis=-1, keepdims=True))
        p_c = mc * jnp.exp(s_c - m)
        p_n = mn * jnp.exp(s_n - m)
        den = jnp.sum(p_c, axis=-1, keepdims=True) + jnp.sum(p_n, axis=-1, keepdims=True)
        num = (_nt(p_c.astype(BF16), vc_ref[0, 0, h].astype(BF16))
               + _dot(p_n.astype(BF16), vn_ref[:, hs].astype(BF16)))
        outs.append(num / den)
    o_ref[...] = jnp.concatenate(outs, axis=-1)


def _attn_sample(q, kn, vn, kc, vc, layer, t):
    m, w = q.shape
    _, b, h, e, lbuf = kc.shape
    for win, d in BRANCHES:
        assert lbuf - d * (win // d) >= 0
    mc, mn = _sample_multiplicity(lbuf, t)
    row = pl.BlockSpec((t, w), lambda i: (i, 0))
    cache = pl.BlockSpec((1, 1, h, e, lbuf), lambda i: (layer, i, 0, 0, 0))
    return pl.pallas_call(
        _attn_sample_kernel,
        grid=(b,),
        in_specs=[row, row, row, cache, cache,
                  _resident((t, lbuf), lambda i: (0, 0)), _resident((t, t), lambda i: (0, 0))],
        out_specs=row,
        out_shape=jax.ShapeDtypeStruct((m, w), F32),
        compiler_params=_params(1),
        name="attn_sample",
    )(q, kn, vn, kc, vc, jnp.asarray(mc), jnp.asarray(mn))


def _discretise(a_re, a_im, log_dt):
    lam_re = jnp.minimum(a_re, LAMBDA_RE_MAX)
    lam_im = a_im
    dt = jnp.exp(log_dt)
    mag = jnp.exp(lam_re * dt)
    ab_re = mag * jnp.cos(lam_im * dt)
    ab_im = mag * jnp.sin(lam_im * dt)
    den = lam_re * lam_re + lam_im * lam_im
    f_re = ((ab_re - 1.0) * lam_re + ab_im * lam_im) / den
    f_im = (ab_im * lam_re - (ab_re - 1.0) * lam_im) / den
    return ab_re, ab_im, f_re, f_im


def _s5_prep_kernel(are_v, aim_v, ldt_v, are_c, aim_c, ldt_c, br_ref, bi_ref, cr_ref, ci_ref,
                    abr_ref, abi_ref, bbr_ref, bbi_ref, crt_ref, cit_ref):
    abr_ref[0], abi_ref[0], _, _ = _discretise(are_v[0], aim_v[0], ldt_v[0])
    _, _, f_re, f_im = _discretise(are_c[0], aim_c[0], ldt_c[0])
    br, bi = br_ref[0], bi_ref[0]
    gc, n = br.shape
    gn = abr_ref.shape[-1]
    spread = (lax.broadcasted_iota(jnp.int32, (n, gn), 1) % n
              == lax.broadcasted_iota(jnp.int32, (n, gn), 0)).astype(BF16)
    own = (lax.broadcasted_iota(jnp.int32, (gc, gn), 1) // n
           == lax.broadcasted_iota(jnp.int32, (gc, gn), 0) // (gc * n // gn))

    def block_diag(compact):
        return jnp.where(own, _dot(compact.astype(BF16), spread), 0.0).astype(BF16)

    bbr_ref[0] = block_diag(f_re * br - f_im * bi)
    bbi_ref[0] = block_diag(f_re * bi + f_im * br)
    crt_ref[0] = block_diag(cr_ref[0])
    cit_ref[0] = block_diag(ci_ref[0])


def _s5_prep(a_re, a_im, log_dt, b_re, b_im, c_re, c_im):
    nl, g, n, c = b_re.shape
    gn, gc = g * n, g * c
    row = lambda a: a.reshape(nl, 1, gn)
    per_channel = lambda a: jnp.repeat(a, c, axis=1)
    ldt = jnp.broadcast_to(log_dt[..., None], (nl, g, n))
    b_t = lambda a: jnp.swapaxes(a, 2, 3).reshape(nl, gc, n)
    vec = pl.BlockSpec((1, 1, gn), lambda l: (l, 0, 0))
    compact = pl.BlockSpec((1, gc, n), lambda l: (l, 0, 0))
    mat = pl.BlockSpec((1, gc, gn), lambda l: (l, 0, 0))
    return pl.pallas_call(
        _s5_prep_kernel,
        grid=(nl,),
        in_specs=[vec] * 3 + [compact] * 7,
        out_specs=[vec, vec] + [mat] * 4,
        out_shape=[jax.ShapeDtypeStruct((nl, 1, gn), F32)] * 2 + [jax.ShapeDtypeStruct((nl, gc, gn), BF16)] * 4,
        compiler_params=_params(1),
        name="s5_prep",
    )(row(a_re), row(a_im), row(ldt), per_channel(a_re), per_channel(a_im), per_channel(ldt),
      b_t(b_re), b_t(b_im), c_re.reshape(nl, gc, n), c_im.reshape(nl, gc, n))


def _s5_kernel(u_ref, h0r_ref, h0i_ref, abr_ref, abi_ref, bbr_ref, bbi_ref, crt_ref, cit_ref, d_ref,
               gw_ref, gb_ref, y_ref, hr_ref, hi_ref, ut, sr, si):
    n_chunk, nb, tc, _ = u_ref.shape
    gn = hr_ref.shape[-1]

    @pl.when(pl.program_id(0) == 0)
    def _():
        hr_ref[...] = h0r_ref[0]
        hi_ref[...] = h0i_ref[0]

    for j in range(n_chunk):
        for b in range(nb):
            ut[j, pl.ds(b, tc, stride=nb), :] = u_ref[j, b]
    u = jnp.concatenate([ut[j] for j in range(n_chunk)], axis=-1)
    ub = u.astype(BF16)
    sw = n_chunk * LANES
    dc, dn = sw // S5_DIAG, gn // S5_DIAG
    for i in range(S5_DIAG):
        ch, st = slice(i * dc, (i + 1) * dc), slice(i * dn, (i + 1) * dn)
        sr[:, st] = _dot(ub[:, ch], bbr_ref[0, ch, st])
        si[:, st] = _dot(ub[:, ch], bbi_ref[0, ch, st])

    cw = min(gn, SCAN_VREGS * SUBLANES * LANES // nb)
    for c0 in range(0, gn, cw):
        cols = slice(c0, c0 + cw)
        ar = jnp.broadcast_to(abr_ref[0, :, cols], (nb, cw))
        ai = jnp.broadcast_to(abi_ref[0, :, cols], (nb, cw))

        def step(t, h, cols=cols, ar=ar, ai=ai):
            hr, hi = h
            rows = pl.ds(pl.multiple_of(t * nb, nb), nb)
            nr = ar * hr - ai * hi + sr[rows, cols]
            ni = ar * hi + ai * hr + si[rows, cols]
            sr[rows, cols] = nr
            si[rows, cols] = ni
            return nr, ni

        hr, hi = lax.fori_loop(0, tc, step, (hr_ref[:, cols], hi_ref[:, cols]))
        hr_ref[:, cols] = hr
        hi_ref[:, cols] = hi

    ch_parts = []
    for i in range(S5_DIAG):
        ch, st = slice(i * dc, (i + 1) * dc), slice(i * dn, (i + 1) * dn)
        ch_parts.append(_nt(sr[:, st].astype(BF16), crt_ref[0, ch, st]) - _nt(si[:, st].astype(BF16), cit_ref[0, ch, st]))
    y = jnp.concatenate(ch_parts, axis=-1) + d_ref[0] * u
    z = jax.nn.gelu(y)
    out = z * jax.nn.sigmoid(_dot(z.astype(BF16), gw_ref[0]) + gb_ref[0])
    for j in range(n_chunk):
        ut[j] = out[:, j * LANES:(j + 1) * LANES]
    for j in range(n_chunk):
        for b in range(nb):
            y_ref[b, :, j * LANES:(j + 1) * LANES] = ut[j, pl.ds(b, tc, stride=nb), :]


def _s5(u, h0r, h0i, abr, abi, bbr, bbi, cr, ci, dskip, gw, gb, layer, h0_layer, nb, tc):
    n_chunk, m, _ = u.shape
    sw = n_chunk * LANES
    t = m // nb
    gn = h0r.shape[-1]
    assert nb % SUBLANES == 0 and tc % SUBLANES == 0
    assert (sw // SSM_GROUP_CH) % S5_DIAG == 0 and (sw // S5_DIAG) % LANES == 0 and (gn // S5_DIAG) % LANES == 0
    lay = lambda *shape: _resident((1,) + shape, lambda i: (layer,) + (0,) * len(shape))
    h0 = _resident((1, nb, gn), lambda i: (h0_layer, 0, 0))
    state = pl.BlockSpec((nb, gn), lambda i: (0, 0))
    y, h_re, h_im = pl.pallas_call(
        _s5_kernel,
        grid=(t // tc,),
        in_specs=[pl.BlockSpec((n_chunk, nb, tc, LANES), lambda i: (0, 0, i, 0)), h0, h0,
                  lay(1, gn), lay(1, gn), lay(sw, gn), lay(sw, gn), lay(sw, gn), lay(sw, gn),
                  lay(1, sw), lay(sw, sw), lay(1, sw)],
        out_specs=[pl.BlockSpec((nb, tc, sw), lambda i: (0, i, 0)), state, state],
        out_shape=[jax.ShapeDtypeStruct((nb, t, sw), F32)] + [jax.ShapeDtypeStruct((nb, gn), F32)] * 2,
        scratch_shapes=[pltpu.VMEM((n_chunk, tc * nb, LANES), F32)] + [pltpu.VMEM((tc * nb, gn), F32)] * 2,
        compiler_params=_params(1),
        name="s5",
    )(u.reshape(n_chunk, nb, t, LANES), h0r, h0i, abr, abi, bbr, bbi, cr, ci, dskip, gw, gb)
    return y.reshape(m, sw), h_re, h_im


def _mix_ffn_kernel(x_ref, a_ref, s_ref, ag_ref, sg_ref, wo_ref, n2_ref, wg_ref, wu_ref, wd_ref, fg_ref,
                    y_ref, *, ff_chunk, final):
    aw = a_ref.shape[-1]
    an = _rms(a_ref[...], ag_ref[0]).astype(BF16)
    sn = _rms(s_ref[...], sg_ref[0]).astype(BF16)
    h = x_ref[...] + (_dot(an, wo_ref[0, :aw, :]) + _dot(sn, wo_ref[0, aw:, :]))
    hn = _rms(h, n2_ref[0]).astype(BF16)
    ffn = None
    for c0 in range(0, wg_ref.shape[-1], ff_chunk):
        cols = slice(c0, c0 + ff_chunk)
        act = jax.nn.silu(_dot(hn, wg_ref[0, :, cols])) * _dot(hn, wu_ref[0, :, cols])
        part = _dot(act.astype(BF16), wd_ref[0, cols, :])
        ffn = part if ffn is None else ffn + part
    y = h + ffn
    if final:
        y = _rms(y, fg_ref[...])
    y_ref[...] = y


def _mix_ffn(x, attn, ssm, ag, sg, wo, n2, wg, wu, wd, fg, layer, final):
    m, dm = x.shape
    aw, sw, dff = attn.shape[-1], ssm.shape[-1], wg.shape[-1]
    ff_chunk = dff // 2
    assert ff_chunk % LANES == 0
    tm = min(m, ROW_TILE)
    row = lambda width: pl.BlockSpec((tm, width), lambda i: (i, 0))
    lay = lambda *shape: _resident((1,) + shape, lambda i: (layer,) + (0,) * len(shape))
    return pl.pallas_call(
        functools.partial(_mix_ffn_kernel, ff_chunk=ff_chunk, final=final),
        grid=(m // tm,),
        in_specs=[row(dm), row(aw), row(sw), lay(1, aw), lay(1, sw), lay(aw + sw, dm), lay(1, dm),
                  lay(dm, dff), lay(dm, dff), lay(dff, dm), _resident((1, dm), lambda i: (0, 0))],
        out_specs=row(dm),
        out_shape=jax.ShapeDtypeStruct((m, dm), F32),
        compiler_params=_params(1),
        name="mix_ffn",
    )(x, attn, ssm, ag, sg, wo, n2, wg, wu, wd, fg)


def kernel(x_prompt, x_sample, cache_attn_k, cache_attn_v, state_ssm_re, state_ssm_im, norm1_g, w_in, attn_out_g, ssm_a_re, ssm_a_im, ssm_log_dt, ssm_b_re, ssm_b_im, ssm_c_re, ssm_c_im, ssm_d, ssm_glu_w, ssm_glu_b, ssm_out_g, w_out, norm2_g, ffn_w_gate, ffn_w_up, ffn_w_down, final_norm_g):
    bp, seq, dm = x_prompt.shape
    bs, tdec, _ = x_sample.shape
    depth, _, lbuf, n_heads, head_dim = cache_attn_k.shape
    _, n_groups, state_n = ssm_a_re.shape
    gn = n_groups * state_n
    aw = n_heads * head_dim
    assert head_dim == HEAD_DIM and state_n == STATE_N and lbuf == min(max(w for w, _ in BRANCHES), lbuf)

    vec = lambda a: a.reshape(depth, 1, -1)
    w_in_b, w_out_b = w_in.astype(BF16), w_out.astype(BF16)
    wg_b, wu_b, wd_b = ffn_w_gate.astype(BF16), ffn_w_up.astype(BF16), ffn_w_down.astype(BF16)
    glu_w_b = ssm_glu_w.astype(BF16)
    s5_mats = _s5_prep(ssm_a_re, ssm_a_im, ssm_log_dt, ssm_b_re, ssm_b_im, ssm_c_re, ssm_c_im)
    n1, n2, ag, sg = vec(norm1_g), vec(norm2_g), vec(attn_out_g), vec(ssm_out_g)
    dsk, gb = vec(ssm_d), vec(ssm_glu_b)
    fg = final_norm_g.reshape(1, dm)
    kc = jnp.transpose(cache_attn_k, (0, 1, 3, 4, 2))
    vc = jnp.transpose(cache_attn_v, (0, 1, 3, 4, 2))
    h0p = jnp.zeros((1, bp, gn), F32)
    h0s_re = state_ssm_re.reshape(depth, bs, gn)
    h0s_im = state_ssm_im.reshape(depth, bs, gn)

    xp = x_prompt.reshape(bp * seq, dm)
    xs = x_sample.reshape(bs * tdec, dm)
    kt = vt = None
    pr, pim, sk, sv, sr, sim = ([] for _ in range(6))
    for l in range(depth):
        final = l == depth - 1
        s5w = (*s5_mats, dsk, glu_w_b, gb)
        ffw = (ag, sg, w_out_b, n2, wg_b, wu_b, wd_b, fg)

        q, k, v, u, kt, vt = _in_proj(xp, n1, w_in_b, l, kv_buffers=((depth, bp, aw, seq), kt, vt), seq=seq)
        attn = _attn_prompt(q, k, v, seq)
        ssm, h_re, h_im = _s5(u, h0p, h0p, *s5w, layer=l, h0_layer=0, nb=bp, tc=SCAN_CHUNK)
        xp = _mix_ffn(xp, attn, ssm, *ffw, layer=l, final=final)
        pr.append(h_re.reshape(bp, n_groups, state_n))
        pim.append(h_im.reshape(bp, n_groups, state_n))

        q, k, v, u = _in_proj(xs, n1, w_in_b, l)
        attn = _attn_sample(q, k, v, kc, vc, l, tdec)
        ssm, h_re, h_im = _s5(u, h0s_re, h0s_im, *s5w, layer=l, h0_layer=l, nb=bs, tc=tdec)
        xs = _mix_ffn(xs, attn, ssm, *ffw, layer=l, final=final)
        sk.append(k.reshape(bs, tdec, n_heads, head_dim))
        sv.append(v.reshape(bs, tdec, n_heads, head_dim))
        sr.append(h_re.reshape(bs, n_groups, state_n))
        sim.append(h_im.reshape(bs, n_groups, state_n))

    window = lambda t: jnp.transpose(t.reshape(depth, bp, n_heads, head_dim, seq), (0, 1, 4, 2, 3))[:, :, -lbuf:]
    return (xp.reshape(bp, seq, dm), xs.reshape(bs, tdec, dm),
            window(kt), window(vt), jnp.stack(pr), jnp.stack(pim),
            jnp.stack(sk), jnp.stack(sv), jnp.stack(sr), jnp.stack(sim))
```

```python
import functools

import numpy as np
import jax
import jax.numpy as jnp
from jax import lax
from jax.experimental import pallas as pl
from jax.experimental.pallas import tpu as pltpu

F32 = jnp.float32
BF16 = jnp.bfloat16

HEAD_DIM = 64
BRANCHES = ((128, 1), (512, 4), (2048, 16))
BRANCH_DIL = tuple(d for _, d in BRANCHES)
BAND = 128
assert all(w // d == BAND for w, d in BRANCHES)
SSM_GROUP_CH = 16
STATE_N = 64
EPS = 1e-6
LAMBDA_RE_MAX = -1e-4

LANES = 128
SUBLANES = 8
MXU_TILE = 256
VMEM_LIMIT = 56 * 1024 * 1024

ROW_TILE = 512
SCAN_CHUNK = 128
SCAN_VREGS = 8
S5_DIAG = 2


def _resident(shape, index_map):
    return pl.BlockSpec(shape, index_map, pipeline_mode=pl.Buffered(1))


def _params(n_axes):
    return pltpu.CompilerParams(dimension_semantics=("arbitrary",) * n_axes,
                                vmem_limit_bytes=VMEM_LIMIT)


def _rms(x, g):
    return x * lax.rsqrt(jnp.mean(x * x, axis=-1, keepdims=True) + EPS) * g


def _nt(a, b):
    return lax.dot_general(a, b, (((1,), (1,)), ((), ())), preferred_element_type=F32)


def _dot(a, b):
    return jnp.dot(a, b, preferred_element_type=F32)


def _in_proj_kernel(x_ref, g_ref, w_ref, *refs, transposed_kv):
    q_ref, k_ref, v_ref, u_ref = refs[-6:-2] if transposed_kv else refs[-4:]
    xn = _rms(x_ref[...], g_ref[0])
    z = _dot(xn.astype(BF16), w_ref[0])
    aw = q_ref.shape[-1]
    q_ref[...] = z[:, :aw]
    k_ref[...] = z[:, aw:2 * aw]
    v_ref[...] = z[:, 2 * aw:3 * aw]
    for j in range(u_ref.shape[0]):
        u_ref[j] = z[:, 3 * aw + j * LANES:3 * aw + (j + 1) * LANES]
    if transposed_kv:
        kt_ref, vt_ref = refs[-2:]
        kt_ref[0, 0] = z[:, aw:2 * aw].T
        vt_ref[0, 0] = z[:, 2 * aw:3 * aw].T


def _in_proj(x, g, w, layer, kv_buffers=None, seq=None):
    m, dm = x.shape
    n = w.shape[-1]
    sw = g.shape[-1] // 2
    aw = (n - sw) // 3
    tm = min(m, ROW_TILE)
    row = lambda width: pl.BlockSpec((tm, width), lambda i: (i, 0))
    in_specs = [row(dm),
                _resident((1, 1, dm), lambda i: (layer, 0, 0)),
                _resident((1, dm, n), lambda i: (layer, 0, 0))]
    out_specs = [row(aw), row(aw), row(aw), pl.BlockSpec((sw // LANES, tm, LANES), lambda i: (0, i, 0))]
    out_shape = [jax.ShapeDtypeStruct((m, aw), F32)] * 3 + [jax.ShapeDtypeStruct((sw // LANES, m, LANES), F32)]
    args = [x, g, w]
    aliases = {}
    if kv_buffers is not None:
        buf_shape, kt, vt = kv_buffers
        tiles = seq // tm
        kv_spec = pl.BlockSpec((1, 1, aw, tm), lambda i: (layer, i // tiles, 0, i % tiles))
        out_specs += [kv_spec, kv_spec]
        out_shape += [jax.ShapeDtypeStruct(buf_shape, F32)] * 2
        if kt is not None:
            in_specs += [pl.BlockSpec(memory_space=pl.ANY)] * 2
            args += [kt, vt]
            aliases = {3: 4, 4: 5}
    return pl.pallas_call(
        functools.partial(_in_proj_kernel, transposed_kv=kv_buffers is not None),
        grid=(m // tm,),
        in_specs=in_specs,
        out_specs=out_specs,
        out_shape=out_shape,
        input_output_aliases=aliases,
        compiler_params=_params(1),
        name="in_proj",
    )(*args)


def _attn_prompt_kernel(q_ref, k_ref, v_ref, o_ref, qlo, qhi, kb, vlo, vhi, a_s, l_s, m_s, s_ring, p_ring, *, seq):
    nblk = seq // BAND
    lane = lax.broadcasted_iota(jnp.int32, (BAND, LANES), 1)
    lo = lane < HEAD_DIM
    row = lax.broadcasted_iota(jnp.int32, (BAND, BAND), 0)
    col = lax.broadcasted_iota(jnp.int32, (BAND, BAND), 1)
    causal = col <= row
    band2 = jnp.concatenate([col >= row, causal], axis=1)
    zero = jnp.zeros((BAND, LANES), F32)

    def natural(d, idx):
        nb = nblk // d
        st = idx // nb + d * BAND * (idx % nb)
        return pl.ds(st, BAND, stride=d) if d > 1 else pl.ds(pl.multiple_of(st, BAND), BAND)

    def packed(idx):
        return pl.ds(pl.multiple_of(idx * BAND, BAND), BAND)

    def stage(bi, d, idx):
        src, dst = natural(d, idx), packed(idx)
        qv = q_ref[0, src, :] * (HEAD_DIM ** -0.5)
        qlo[bi, dst, :] = jnp.where(lo, qv, zero).astype(BF16)
        qhi[bi, dst, :] = jnp.where(lo, zero, qv).astype(BF16)
        kb[bi, dst, :] = k_ref[0, src, :].astype(BF16)
        vv = v_ref[0, src, :]
        vlo[bi, dst, :] = jnp.where(lo, vv, zero).astype(BF16)
        vhi[bi, dst, :] = jnp.where(lo, zero, vv).astype(BF16)

    def slab(d, idx):
        has_prev = idx % (nblk // d) > 0
        return (pl.ds((idx - 1) * BAND, 2 * BAND) if has_prev else pl.ds(idx * BAND, BAND)), has_prev

    def scores(bi, d, idx, slot):
        rows, _ = slab(d, idx)
        keys = kb[bi, rows, :]
        for h, qr in enumerate((qlo, qhi)):
            s_ring[slot, h, :, :keys.shape[0]] = _nt(qr[bi, pl.ds(idx * BAND, BAND), :], keys)

    def softmax(bi, d, idx, slot):
        _, has_prev = slab(d, idx)
        ms = []
        for h in range(2):
            if has_prev:
                s = jnp.where(band2, s_ring[slot, h], -jnp.inf)
                m = jnp.max(jnp.maximum(s[:, :BAND], s[:, BAND:]), axis=-1, keepdims=True)
                p_ring[slot, h] = jnp.exp(s - m).astype(BF16)
            else:
                s = jnp.where(causal, s_ring[slot, h, :, :BAND], -jnp.inf)
                m = jnp.max(s, axis=-1, keepdims=True)
                p_ring[slot, h, :, :BAND] = jnp.exp(s - m).astype(BF16)
            ms.append(m)
        m_s[bi, natural(d, idx), :] = jnp.where(lo, ms[0], ms[1])

    def values(bi, d, idx, slot):
        rows, has_prev = slab(d, idx)
        span = 2 * BAND if has_prev else BAND
        ones = jnp.ones((span, LANES), BF16)
        accs = [_dot(p_ring[slot, h, :, :span], jnp.concatenate([vr[bi, rows, :], ones], axis=1))
                for h, vr in enumerate((vlo, vhi))]
        dst = natural(d, idx)
        a_s[bi, dst, :] = accs[0][:, :LANES] + accs[1][:, :LANES]
        l_s[bi, dst, :] = jnp.where(lo, accs[0][:, LANES:], accs[1][:, LANES:])

    blocks = [(bi, d, idx) for bi, d in enumerate(BRANCH_DIL) for idx in range(nblk)]
    stage(*blocks[0])
    for t in range(len(blocks) + 2):
        if t >= 2:
            values(*blocks[t - 2], t % 2)
        if 1 <= t <= len(blocks):
            softmax(*blocks[t - 1], (t - 1) % 2)
        if t < len(blocks):
            scores(*blocks[t], t % 2)
        if t + 1 < len(blocks):
            stage(*blocks[t + 1])

    def combine(i, carry):
        sl = pl.ds(pl.multiple_of(i * BAND, BAND), BAND)
        ms = [m_s[bi, sl, :] for bi in range(len(BRANCH_DIL))]
        mm = functools.reduce(jnp.maximum, ms)
        fs = [jnp.exp(m - mm) for m in ms]
        num = sum(f * a_s[bi, sl, :] for bi, f in enumerate(fs))
        den = sum(f * l_s[bi, sl, :] for bi, f in enumerate(fs))
        o_ref[0, sl, :] = num / den
        return carry

    lax.fori_loop(0, nblk, combine, 0, unroll=2)


def _attn_prompt(q, k, v, seq):
    m, w = q.shape
    b = m // seq
    assert seq % (BAND * max(BRANCH_DIL)) == 0 and w % LANES == 0
    spec = pl.BlockSpec((1, seq, LANES), lambda i, j: (i, 0, j))
    nbr = len(BRANCH_DIL)
    out = pl.pallas_call(
        functools.partial(_attn_prompt_kernel, seq=seq),
        grid=(b, w // LANES),
        in_specs=[spec, spec, spec],
        out_specs=spec,
        out_shape=jax.ShapeDtypeStruct((b, seq, w), F32),
        scratch_shapes=([pltpu.VMEM((nbr, seq, LANES), BF16)] * 5 + [pltpu.VMEM((nbr, seq, LANES), F32)] * 3
                        + [pltpu.VMEM((2, 2, BAND, 2 * BAND), F32), pltpu.VMEM((2, 2, BAND, 2 * BAND), BF16)]),
        compiler_params=_params(2),
        name="attn_prompt",
    )(*(t.reshape(b, seq, w) for t in (q, k, v)))
    return out.reshape(m, w)


def _sample_multiplicity(lbuf, t):
    dist_c = lbuf + np.arange(t)[:, None] - np.arange(lbuf)[None, :]
    dist_n = np.arange(t)[:, None] - np.arange(t)[None, :]
    mc = np.zeros((t, lbuf), np.float32)
    mn = np.zeros((t, t), np.float32)
    for w, d in BRANCHES:
        mc += (dist_c % d == 0) & (dist_c <= w)
        mn += (dist_n >= 0) & (dist_n % d == 0) & (dist_n <= w)
    return mc, mn


def _attn_sample_kernel(q_ref, kn_ref, vn_ref, kc_ref, vc_ref, mc_ref, mn_ref, o_ref):
    n_heads = kc_ref.shape[2]
    mc = mc_ref[...]
    mn = mn_ref[...]
    outs = []
    for h in range(n_heads):
        hs = slice(h * HEAD_DIM, (h + 1) * HEAD_DIM)
        qh = (q_ref[:, hs] * (HEAD_DIM ** -0.5)).astype(BF16)
        s_c = jnp.where(mc > 0, _dot(qh, kc_ref[0, 0, h].astype(BF16)), -jnp.inf)
        s_n = jnp.where(mn > 0, _nt(qh, kn_ref[:, hs].astype(BF16)), -jnp.inf)
        m = jnp.maximum(jnp.max(s_c, axis=-1, keepdims=True), jnp.max(s_n, axis=-1, keepdims=True))
        p_c = mc * jnp.exp(s_c - m)
        p_n = mn * jnp.exp(s_n - m)
        den = jnp.sum(p_c, axis=-1, keepdims=True) + jnp.sum(p_n, axis=-1, keepdims=True)
        num = (_nt(p_c.astype(BF16), vc_ref[0, 0, h].astype(BF16))
               + _dot(p_n.astype(BF16), vn_ref[:, hs].astype(BF16)))
        outs.append(num / den)
    o_ref[...] = jnp.concatenate(outs, axis=-1)


def _attn_sample(q, kn, vn, kc, vc, layer, t):
    m, w = q.shape
    _, b, h, e, lbuf = kc.shape
    for win, d in BRANCHES:
        assert lbuf - d * (win // d) >= 0
    mc, mn = _sample_multiplicity(lbuf, t)
    row = pl.BlockSpec((t, w), lambda i: (i, 0))
    cache = pl.BlockSpec((1, 1, h, e, lbuf), lambda i: (layer, i, 0, 0, 0))
    return pl.pallas_call(
        _attn_sample_kernel,
        grid=(b,),
        in_specs=[row, row, row, cache, cache,
                  _resident((t, lbuf), lambda i: (0, 0)), _resident((t, t), lambda i: (0, 0))],
        out_specs=row,
        out_shape=jax.ShapeDtypeStruct((m, w), F32),
        compiler_params=_params(1),
        name="attn_sample",
    )(q, kn, vn, kc, vc, jnp.asarray(mc), jnp.asarray(mn))


def _discretise(a_re, a_im, log_dt):
    lam_re = jnp.minimum(a_re, LAMBDA_RE_MAX)
    lam_im = a_im
    dt = jnp.exp(log_dt)
    mag = jnp.exp(lam_re * dt)
    ab_re = mag * jnp.cos(lam_im * dt)
    ab_im = mag * jnp.sin(lam_im * dt)
    den = lam_re * lam_re + lam_im * lam_im
    f_re = ((ab_re - 1.0) * lam_re + ab_im * lam_im) / den
    f_im = (ab_im * lam_re - (ab_re - 1.0) * lam_im) / den
    return ab_re, ab_im, f_re, f_im


def _s5_prep_kernel(are_v, aim_v, ldt_v, are_c, aim_c, ldt_c, br_ref, bi_ref, cr_ref, ci_ref,
                    abr_ref, abi_ref, bbr_ref, bbi_ref, crt_ref, cit_ref):
    abr_ref[0], abi_ref[0], _, _ = _discretise(are_v[0], aim_v[0], ldt_v[0])
    _, _, f_re, f_im = _discretise(are_c[0], aim_c[0], ldt_c[0])
    br, bi = br_ref[0], bi_ref[0]
    gc, n = br.shape
    gn = abr_ref.shape[-1]
    spread = (lax.broadcasted_iota(jnp.int32, (n, gn), 1) % n
              == lax.broadcasted_iota(jnp.int32, (n, gn), 0)).astype(BF16)
    own = (lax.broadcasted_iota(jnp.int32, (gc, gn), 1) // n
           == lax.broadcasted_iota(jnp.int32, (gc, gn), 0) // (gc * n // gn))

    def block_diag(compact):
        return jnp.where(own, _dot(compact.astype(BF16), spread), 0.0).astype(BF16)

    bbr_ref[0] = block_diag(f_re * br - f_im * bi)
    bbi_ref[0] = block_diag(f_re * bi + f_im * br)
    crt_ref[0] = block_diag(cr_ref[0])
    cit_ref[0] = block_diag(ci_ref[0])


def _s5_prep(a_re, a_im, log_dt, b_re, b_im, c_re, c_im):
    nl, g, n, c = b_re.shape
    gn, gc = g * n, g * c
    row = lambda a: a.reshape(nl, 1, gn)
    per_channel = lambda a: jnp.repeat(a, c, axis=1)
    ldt = jnp.broadcast_to(log_dt[..., None], (nl, g, n))
    b_t = lambda a: jnp.swapaxes(a, 2, 3).reshape(nl, gc, n)
    vec = pl.BlockSpec((1, 1, gn), lambda l: (l, 0, 0))
    compact = pl.BlockSpec((1, gc, n), lambda l: (l, 0, 0))
    mat = pl.BlockSpec((1, gc, gn), lambda l: (l, 0, 0))
    return pl.pallas_call(
        _s5_prep_kernel,
        grid=(nl,),
        in_specs=[vec] * 3 + [compact] * 7,
        out_specs=[vec, vec] + [mat] * 4,
        out_shape=[jax.ShapeDtypeStruct((nl, 1, gn), F32)] * 2 + [jax.ShapeDtypeStruct((nl, gc, gn), BF16)] * 4,
        compiler_params=_params(1),
        name="s5_prep",
    )(row(a_re), row(a_im), row(ldt), per_channel(a_re), per_channel(a_im), per_channel(ldt),
      b_t(b_re), b_t(b_im), c_re.reshape(nl, gc, n), c_im.reshape(nl, gc, n))


def _s5_kernel(u_ref, h0r_ref, h0i_ref, abr_ref, abi_ref, bbr_ref, bbi_ref, crt_ref, cit_ref, d_ref,
               gw_ref, gb_ref, y_ref, hr_ref, hi_ref, ut, sr, si):
    n_chunk, nb, tc, _ = u_ref.shape
    gn = hr_ref.shape[-1]

    @pl.when(pl.program_id(0) == 0)
    def _():
        hr_ref[...] = h0r_ref[0]
        hi_ref[...] = h0i_ref[0]

    for j in range(n_chunk):
        for b in range(nb):
            ut[j, pl.ds(b, tc, stride=nb), :] = u_ref[j, b]
    u = jnp.concatenate([ut[j] for j in range(n_chunk)], axis=-1)
    ub = u.astype(BF16)
    sw = n_chunk * LANES
    dc, dn = sw // S5_DIAG, gn // S5_DIAG
    for i in range(S5_DIAG):
        ch, st = slice(i * dc, (i + 1) * dc), slice(i * dn, (i + 1) * dn)
        sr[:, st] = _dot(ub[:, ch], bbr_ref[0, ch, st])
        si[:, st] = _dot(ub[:, ch], bbi_ref[0, ch, st])

    cw = min(gn, SCAN_VREGS * SUBLANES * LANES // nb)
    for c0 in range(0, gn, cw):
        cols = slice(c0, c0 + cw)
        ar = jnp.broadcast_to(abr_ref[0, :, cols], (nb, cw))
        ai = jnp.broadcast_to(abi_ref[0, :, cols], (nb, cw))

        def step(t, h, cols=cols, ar=ar, ai=ai):
            hr, hi = h
            rows = pl.ds(pl.multiple_of(t * nb, nb), nb)
            nr = ar * hr - ai * hi + sr[rows, cols]
            ni = ar * hi + ai * hr + si[rows, cols]
            sr[rows, cols] = nr
            si[rows, cols] = ni
            return nr, ni

        hr, hi = lax.fori_loop(0, tc, step, (hr_ref[:, cols], hi_ref[:, cols]))
        hr_ref[:, cols] = hr
        hi_ref[:, cols] = hi

    ch_parts = []
    for i in range(S5_DIAG):
        ch, st = slice(i * dc, (i + 1) * dc), slice(i * dn, (i + 1) * dn)
        ch_parts.append(_nt(sr[:, st].astype(BF16), crt_ref[0, ch, st]) - _nt(si[:, st].astype(BF16), cit_ref[0, ch, st]))
    y = jnp.concatenate(ch_parts, axis=-1) + d_ref[0] * u
    z = jax.nn.gelu(y)
    out = z * jax.nn.sigmoid(_dot(z.astype(BF16), gw_ref[0]) + gb_ref[0])
    for j in range(n_chunk):
        ut[j] = out[:, j * LANES:(j + 1) * LANES]
    for j in range(n_chunk):
        for b in range(nb):
            y_ref[b, :, j * LANES:(j + 1) * LANES] = ut[j, pl.ds(b, tc, stride=nb), :]


def _s5(u, h0r, h0i, abr, abi, bbr, bbi, cr, ci, dskip, gw, gb, layer, h0_layer, nb, tc):
    n_chunk, m, _ = u.shape
    sw = n_chunk * LANES
    t = m // nb
    gn = h0r.shape[-1]
    assert nb % SUBLANES == 0 and tc % SUBLANES == 0
    assert (sw // SSM_GROUP_CH) % S5_DIAG == 0 and (sw // S5_DIAG) % LANES == 0 and (gn // S5_DIAG) % LANES == 0
    lay = lambda *shape: _resident((1,) + shape, lambda i: (layer,) + (0,) * len(shape))
    h0 = _resident((1, nb, gn), lambda i: (h0_layer, 0, 0))
    state = pl.BlockSpec((nb, gn), lambda i: (0, 0))
    y, h_re, h_im = pl.pallas_call(
        _s5_kernel,
        grid=(t // tc,),
        in_specs=[pl.BlockSpec((n_chunk, nb, tc, LANES), lambda i: (0, 0, i, 0)), h0, h0,
                  lay(1, gn), lay(1, gn), lay(sw, gn), lay(sw, gn), lay(sw, gn), lay(sw, gn),
                  lay(1, sw), lay(sw, sw), lay(1, sw)],
        out_specs=[pl.BlockSpec((nb, tc, sw), lambda i: (0, i, 0)), state, state],
        out_shape=[jax.ShapeDtypeStruct((nb, t, sw), F32)] + [jax.ShapeDtypeStruct((nb, gn), F32)] * 2,
        scratch_shapes=[pltpu.VMEM((n_chunk, tc * nb, LANES), F32)] + [pltpu.VMEM((tc * nb, gn), F32)] * 2,
        compiler_params=_params(1),
        name="s5",
    )(u.reshape(n_chunk, nb, t, LANES), h0r, h0i, abr, abi, bbr, bbi, cr, ci, dskip, gw, gb)
    return y.reshape(m, sw), h_re, h_im


def _mix_ffn_kernel(x_ref, a_ref, s_ref, ag_ref, sg_ref, wo_ref, n2_ref, wg_ref, wu_ref, wd_ref, fg_ref,
                    y_ref, *, ff_split, final):
    aw = a_ref.shape[-1]
    an = _rms(a_ref[...], ag_ref[0]).astype(BF16)
    sn = _rms(s_ref[...], sg_ref[0]).astype(BF16)
    h = x_ref[...] + (_dot(an, wo_ref[0, :aw, :]) + _dot(sn, wo_ref[0, aw:, :]))
    hn = _rms(h, n2_ref[0]).astype(BF16)
    ffn = None
    for cols in (slice(0, ff_split), slice(ff_split, wg_ref.shape[-1])):
        act = jax.nn.silu(_dot(hn, wg_ref[0, :, cols])) * _dot(hn, wu_ref[0, :, cols])
        part = _dot(act.astype(BF16), wd_ref[0, cols, :])
        ffn = part if ffn is None else ffn + part
    y = h + ffn
    if final:
        y = _rms(y, fg_ref[...])
    y_ref[...] = y


def _mix_ffn(x, attn, ssm, ag, sg, wo, n2, wg, wu, wd, fg, layer, final):
    m, dm = x.shape
    aw, sw, dff = attn.shape[-1], ssm.shape[-1], wg.shape[-1]
    ff_split = pl.cdiv(pl.cdiv(dff, MXU_TILE), 2) * MXU_TILE
    assert 0 < ff_split < dff and dff % LANES == 0
    tm = min(m, ROW_TILE)
    row = lambda width: pl.BlockSpec((tm, width), lambda i: (i, 0))
    lay = lambda *shape: _resident((1,) + shape, lambda i: (layer,) + (0,) * len(shape))
    return pl.pallas_call(
        functools.partial(_mix_ffn_kernel, ff_split=ff_split, final=final),
        grid=(m // tm,),
        in_specs=[row(dm), row(aw), row(sw), lay(1, aw), lay(1, sw), lay(aw + sw, dm), lay(1, dm),
                  lay(dm, dff), lay(dm, dff), lay(dff, dm), _resident((1, dm), lambda i: (0, 0))],
        out_specs=row(dm),
        out_shape=jax.ShapeDtypeStruct((m, dm), F32),
        compiler_params=_params(1),
        name="mix_ffn",
    )(x, attn, ssm, ag, sg, wo, n2, wg, wu, wd, fg)


def kernel(x_prompt, x_sample, cache_attn_k, cache_attn_v, state_ssm_re, state_ssm_im, norm1_g, w_in, attn_out_g, ssm_a_re, ssm_a_im, ssm_log_dt, ssm_b_re, ssm_b_im, ssm_c_re, ssm_c_im, ssm_d, ssm_glu_w, ssm_glu_b, ssm_out_g, w_out, norm2_g, ffn_w_gate, ffn_w_up, ffn_w_down, final_norm_g):
    bp, seq, dm = x_prompt.shape
    bs, tdec, _ = x_sample.shape
    depth, _, lbuf, n_heads, head_dim = cache_attn_k.shape
    _, n_groups, state_n = ssm_a_re.shape
    gn = n_groups * state_n
    aw = n_heads * head_dim
    assert head_dim == HEAD_DIM and state_n == STATE_N and lbuf == min(max(w for w, _ in BRANCHES), lbuf)

    vec = lambda a: a.reshape(depth, 1, -1)
    w_in_b, w_out_b = w_in.astype(BF16), w_out.astype(BF16)
    wg_b, wu_b, wd_b = ffn_w_gate.astype(BF16), ffn_w_up.astype(BF16), ffn_w_down.astype(BF16)
    glu_w_b = ssm_glu_w.astype(BF16)
    s5_mats = _s5_prep(ssm_a_re, ssm_a_im, ssm_log_dt, ssm_b_re, ssm_b_im, ssm_c_re, ssm_c_im)
    n1, n2, ag, sg = vec(norm1_g), vec(norm2_g), vec(attn_out_g), vec(ssm_out_g)
    dsk, gb = vec(ssm_d), vec(ssm_glu_b)
    fg = final_norm_g.reshape(1, dm)
    kc = jnp.transpose(cache_attn_k, (0, 1, 3, 4, 2))
    vc = jnp.transpose(cache_attn_v, (0, 1, 3, 4, 2))
    h0p = jnp.zeros((1, bp, gn), F32)
    h0s_re = state_ssm_re.reshape(depth, bs, gn)
    h0s_im = state_ssm_im.reshape(depth, bs, gn)

    xp = x_prompt.reshape(bp * seq, dm)
    xs = x_sample.reshape(bs * tdec, dm)
    kt = vt = None
    pr, pim, sk, sv, sr, sim = ([] for _ in range(6))
    for l in range(depth):
        final = l == depth - 1
        s5w = (*s5_mats, dsk, glu_w_b, gb)
        ffw = (ag, sg, w_out_b, n2, wg_b, wu_b, wd_b, fg)

        q, k, v, u, kt, vt = _in_proj(xp, n1, w_in_b, l, kv_buffers=((depth, bp, aw, seq), kt, vt), seq=seq)
        attn = _attn_prompt(q, k, v, seq)
        ssm, h_re, h_im = _s5(u, h0p, h0p, *s5w, layer=l, h0_layer=0, nb=bp, tc=SCAN_CHUNK)
        xp = _mix_ffn(xp, attn, ssm, *ffw, layer=l, final=final)
        pr.append(h_re.reshape(bp, n_groups, state_n))
        pim.append(h_im.reshape(bp, n_groups, state_n))

        q, k, v, u = _in_proj(xs, n1, w_in_b, l)
        attn = _attn_sample(q, k, v, kc, vc, l, tdec)
        ssm, h_re, h_im = _s5(u, h0s_re, h0s_im, *s5w, layer=l, h0_layer=l, nb=bs, tc=tdec)
        xs = _mix_ffn(xs, attn, ssm, *ffw, layer=l, final=final)
        sk.append(k.reshape(bs, tdec, n_heads, head_dim))
        sv.append(v.reshape(bs, tdec, n_heads, head_dim))
        sr.append(h_re.reshape(bs, n_groups, state_n))
        sim.append(h_im.reshape(bs, n_groups, state_n))

    window = lambda t: jnp.transpose(t.reshape(depth, bp, n_heads, head_dim, seq), (0, 1, 4, 2, 3))[:, :, -lbuf:]
    return (xp.reshape(bp, seq, dm), xs.reshape(bs, tdec, dm),
            window(kt), window(vt), jnp.stack(pr), jnp.stack(pim),
            jnp.stack(sk), jnp.stack(sv), jnp.stack(sr), jnp.stack(sim))
```

```python
import functools

import numpy as np
import jax
import jax.numpy as jnp
from jax import lax
from jax.experimental import pallas as pl
from jax.experimental.pallas import tpu as pltpu

F32 = jnp.float32
BF16 = jnp.bfloat16

HEAD_DIM = 64
BRANCHES = ((128, 1), (512, 4), (2048, 16))
BRANCH_DIL = tuple(d for _, d in BRANCHES)
BAND = 128
assert all(w // d == BAND for w, d in BRANCHES)
SSM_GROUP_CH = 16
STATE_N = 64
EPS = 1e-6
LAMBDA_RE_MAX = -1e-4

LANES = 128
SUBLANES = 8
MXU_TILE = 256
VMEM_LIMIT = 56 * 1024 * 1024

IN_PROJ_ROW_TILE = 1024
FFN_ROW_TILE = 512
SCAN_CHUNK = 128
SCAN_VREGS = 8
S5_DIAG = 2


def _resident(shape, index_map):
    return pl.BlockSpec(shape, index_map, pipeline_mode=pl.Buffered(1))


def _params(n_axes):
    return pltpu.CompilerParams(dimension_semantics=("arbitrary",) * n_axes,
                                vmem_limit_bytes=VMEM_LIMIT)


def _rms(x, g):
    return x * lax.rsqrt(jnp.mean(x * x, axis=-1, keepdims=True) + EPS) * g


def _nt(a, b):
    return lax.dot_general(a, b, (((1,), (1,)), ((), ())), preferred_element_type=F32)


def _dot(a, b):
    return jnp.dot(a, b, preferred_element_type=F32)


def _in_proj_kernel(x_ref, g_ref, w_ref, *refs, transposed_kv):
    q_ref, k_ref, v_ref, u_ref = refs[-6:-2] if transposed_kv else refs[-4:]
    xn = _rms(x_ref[...], g_ref[0])
    z = _dot(xn.astype(BF16), w_ref[0])
    aw = q_ref.shape[-1]
    q_ref[...] = z[:, :aw]
    k_ref[...] = z[:, aw:2 * aw]
    v_ref[...] = z[:, 2 * aw:3 * aw]
    for j in range(u_ref.shape[0]):
        u_ref[j] = z[:, 3 * aw + j * LANES:3 * aw + (j + 1) * LANES]
    if transposed_kv:
        kt_ref, vt_ref = refs[-2:]
        kt_ref[0, 0] = z[:, aw:2 * aw].T
        vt_ref[0, 0] = z[:, 2 * aw:3 * aw].T


def _in_proj(x, g, w, layer, kv_buffers=None, seq=None):
    m, dm = x.shape
    n = w.shape[-1]
    sw = g.shape[-1] // 2
    aw = (n - sw) // 3
    tm = min(m, IN_PROJ_ROW_TILE)
    row = lambda width: pl.BlockSpec((tm, width), lambda i: (i, 0))
    in_specs = [row(dm),
                _resident((1, 1, dm), lambda i: (layer, 0, 0)),
                _resident((1, dm, n), lambda i: (layer, 0, 0))]
    out_specs = [row(aw), row(aw), row(aw), pl.BlockSpec((sw // LANES, tm, LANES), lambda i: (0, i, 0))]
    out_shape = [jax.ShapeDtypeStruct((m, aw), F32)] * 3 + [jax.ShapeDtypeStruct((sw // LANES, m, LANES), F32)]
    args = [x, g, w]
    aliases = {}
    if kv_buffers is not None:
        buf_shape, kt, vt = kv_buffers
        assert seq % tm == 0
        tiles = seq // tm
        kv_spec = pl.BlockSpec((1, 1, aw, tm), lambda i: (layer, i // tiles, 0, i % tiles))
        out_specs += [kv_spec, kv_spec]
        out_shape += [jax.ShapeDtypeStruct(buf_shape, F32)] * 2
        if kt is not None:
            in_specs += [pl.BlockSpec(memory_space=pl.ANY)] * 2
            args += [kt, vt]
            aliases = {3: 4, 4: 5}
    return pl.pallas_call(
        functools.partial(_in_proj_kernel, transposed_kv=kv_buffers is not None),
        grid=(m // tm,),
        in_specs=in_specs,
        out_specs=out_specs,
        out_shape=out_shape,
        input_output_aliases=aliases,
        compiler_params=_params(1),
        name="in_proj",
    )(*args)


def _attn_prompt_kernel(q_ref, k_ref, v_ref, o_ref, qlo, qhi, kb, vlo, vhi, a_s, l_s, m_s, s_ring, p_ring, *, seq):
    nblk = seq // BAND
    lane = lax.broadcasted_iota(jnp.int32, (BAND, LANES), 1)
    lo = lane < HEAD_DIM
    row = lax.broadcasted_iota(jnp.int32, (BAND, BAND), 0)
    col = lax.broadcasted_iota(jnp.int32, (BAND, BAND), 1)
    causal = col <= row
    band2 = jnp.concatenate([col >= row, causal], axis=1)
    zero = jnp.zeros((BAND, LANES), F32)

    def natural(d, idx):
        nb = nblk // d
        st = idx // nb + d * BAND * (idx % nb)
        return pl.ds(st, BAND, stride=d) if d > 1 else pl.ds(pl.multiple_of(st, BAND), BAND)

    def packed(idx):
        return pl.ds(pl.multiple_of(idx * BAND, BAND), BAND)

    def stage(bi, d, idx):
        src, dst = natural(d, idx), packed(idx)
        qv = q_ref[0, src, :] * (HEAD_DIM ** -0.5)
        qlo[bi, dst, :] = jnp.where(lo, qv, zero).astype(BF16)
        qhi[bi, dst, :] = jnp.where(lo, zero, qv).astype(BF16)
        kb[bi, dst, :] = k_ref[0, src, :].astype(BF16)
        vv = v_ref[0, src, :]
        vlo[bi, dst, :] = jnp.where(lo, vv, zero).astype(BF16)
        vhi[bi, dst, :] = jnp.where(lo, zero, vv).astype(BF16)

    def slab(d, idx):
        has_prev = idx % (nblk // d) > 0
        return (pl.ds((idx - 1) * BAND, 2 * BAND) if has_prev else pl.ds(idx * BAND, BAND)), has_prev

    def scores(bi, d, idx, slot):
        rows, _ = slab(d, idx)
        keys = kb[bi, rows, :]
        for h, qr in enumerate((qlo, qhi)):
            s_ring[slot, h, :, :keys.shape[0]] = _nt(qr[bi, pl.ds(idx * BAND, BAND), :], keys)

    def softmax(bi, d, idx, slot):
        _, has_prev = slab(d, idx)
        ms = []
        for h in range(2):
            if has_prev:
                s = jnp.where(band2, s_ring[slot, h], -jnp.inf)
                m = jnp.max(jnp.maximum(s[:, :BAND], s[:, BAND:]), axis=-1, keepdims=True)
                p_ring[slot, h] = jnp.exp(s - m).astype(BF16)
            else:
                s = jnp.where(causal, s_ring[slot, h, :, :BAND], -jnp.inf)
                m = jnp.max(s, axis=-1, keepdims=True)
                p_ring[slot, h, :, :BAND] = jnp.exp(s - m).astype(BF16)
            ms.append(m)
        m_s[bi, natural(d, idx), :] = jnp.where(lo, ms[0], ms[1])

    def values(bi, d, idx, slot):
        rows, has_prev = slab(d, idx)
        span = 2 * BAND if has_prev else BAND
        ones = jnp.ones((span, LANES), BF16)
        accs = [_dot(p_ring[slot, h, :, :span], jnp.concatenate([vr[bi, rows, :], ones], axis=1))
                for h, vr in enumerate((vlo, vhi))]
        dst = natural(d, idx)
        a_s[bi, dst, :] = accs[0][:, :LANES] + accs[1][:, :LANES]
        l_s[bi, dst, :] = jnp.where(lo, accs[0][:, LANES:], accs[1][:, LANES:])

    blocks = [(bi, d, idx) for bi, d in enumerate(BRANCH_DIL) for idx in range(nblk)]
    stage(*blocks[0])
    for t in range(len(blocks) + 2):
        if t >= 2:
            values(*blocks[t - 2], t % 2)
        if 1 <= t <= len(blocks):
            softmax(*blocks[t - 1], (t - 1) % 2)
        if t < len(blocks):
            scores(*blocks[t], t % 2)
        if t + 1 < len(blocks):
            stage(*blocks[t + 1])

    def combine(i, carry):
        sl = pl.ds(pl.multiple_of(i * BAND, BAND), BAND)
        ms = [m_s[bi, sl, :] for bi in range(len(BRANCH_DIL))]
        mm = functools.reduce(jnp.maximum, ms)
        fs = [jnp.exp(m - mm) for m in ms]
        num = sum(f * a_s[bi, sl, :] for bi, f in enumerate(fs))
        den = sum(f * l_s[bi, sl, :] for bi, f in enumerate(fs))
        o_ref[0, sl, :] = num / den
        return carry

    lax.fori_loop(0, nblk, combine, 0, unroll=2)


def _attn_prompt(q, k, v, seq):
    m, w = q.shape
    b = m // seq
    assert seq % (BAND * max(BRANCH_DIL)) == 0 and w % LANES == 0
    spec = pl.BlockSpec((1, seq, LANES), lambda i, j: (i, 0, j))
    nbr = len(BRANCH_DIL)
    out = pl.pallas_call(
        functools.partial(_attn_prompt_kernel, seq=seq),
        grid=(b, w // LANES),
        in_specs=[spec, spec, spec],
        out_specs=spec,
        out_shape=jax.ShapeDtypeStruct((b, seq, w), F32),
        scratch_shapes=([pltpu.VMEM((nbr, seq, LANES), BF16)] * 5 + [pltpu.VMEM((nbr, seq, LANES), F32)] * 3
                        + [pltpu.VMEM((2, 2, BAND, 2 * BAND), F32), pltpu.VMEM((2, 2, BAND, 2 * BAND), BF16)]),
        compiler_params=_params(2),
        name="attn_prompt",
    )(*(t.reshape(b, seq, w) for t in (q, k, v)))
    return out.reshape(m, w)


def _sample_multiplicity(lbuf, t):
    dist_c = lbuf + np.arange(t)[:, None] - np.arange(lbuf)[None, :]
    dist_n = np.arange(t)[:, None] - np.arange(t)[None, :]
    mc = np.zeros((t, lbuf), np.float32)
    mn = np.zeros((t, t), np.float32)
    for w, d in BRANCHES:
        mc += (dist_c % d == 0) & (dist_c <= w)
        mn += (dist_n >= 0) & (dist_n % d == 0) & (dist_n <= w)
    return mc, mn


def _attn_sample_kernel(q_ref, kn_ref, vn_ref, kc_ref, vc_ref, mc_ref, mn_ref, o_ref):
    n_heads = kc_ref.shape[2]
    mc = mc_ref[...]
    mn = mn_ref[...]
    outs = []
    for h in range(n_heads):
        hs = slice(h * HEAD_DIM, (h + 1) * HEAD_DIM)
        qh = (q_ref[:, hs] * (HEAD_DIM ** -0.5)).astype(BF16)
        s_c = jnp.where(mc > 0, _dot(qh, kc_ref[0, 0, h].astype(BF16)), -jnp.inf)
        s_n = jnp.where(mn > 0, _nt(qh, kn_ref[:, hs].astype(BF16)), -jnp.inf)
        m = jnp.maximum(jnp.max(s_c, axis=-1, keepdims=True), jnp.max(s_n, axis=-1, keepdims=True))
        p_c = mc * jnp.exp(s_c - m)
        p_n = mn * jnp.exp(s_n - m)
        den = jnp.sum(p_c, axis=-1, keepdims=True) + jnp.sum(p_n, axis=-1, keepdims=True)
        num = (_nt(p_c.astype(BF16), vc_ref[0, 0, h].astype(BF16))
               + _dot(p_n.astype(BF16), vn_ref[:, hs].astype(BF16)))
        outs.append(num / den)
    o_ref[...] = jnp.concatenate(outs, axis=-1)


def _attn_sample(q, kn, vn, kc, vc, layer, t):
    m, w = q.shape
    _, b, h, e, lbuf = kc.shape
    for win, d in BRANCHES:
        assert lbuf - d * (win // d) >= 0
    mc, mn = _sample_multiplicity(lbuf, t)
    row = pl.BlockSpec((t, w), lambda i: (i, 0))
    cache = pl.BlockSpec((1, 1, h, e, lbuf), lambda i: (layer, i, 0, 0, 0))
    return pl.pallas_call(
        _attn_sample_kernel,
        grid=(b,),
        in_specs=[row, row, row, cache, cache,
                  _resident((t, lbuf), lambda i: (0, 0)), _resident((t, t), lambda i: (0, 0))],
        out_specs=row,
        out_shape=jax.ShapeDtypeStruct((m, w), F32),
        compiler_params=_params(1),
        name="attn_sample",
    )(q, kn, vn, kc, vc, jnp.asarray(mc), jnp.asarray(mn))


def _discretise(a_re, a_im, log_dt):
    lam_re = jnp.minimum(a_re, LAMBDA_RE_MAX)
    lam_im = a_im
    dt = jnp.exp(log_dt)
    mag = jnp.exp(lam_re * dt)
    ab_re = mag * jnp.cos(lam_im * dt)
    ab_im = mag * jnp.sin(lam_im * dt)
    den = lam_re * lam_re + lam_im * lam_im
    f_re = ((ab_re - 1.0) * lam_re + ab_im * lam_im) / den
    f_im = (ab_im * lam_re - (ab_re - 1.0) * lam_im) / den
    return ab_re, ab_im, f_re, f_im


def _s5_prep_kernel(are_v, aim_v, ldt_v, are_c, aim_c, ldt_c, br_ref, bi_ref, cr_ref, ci_ref,
                    abr_ref, abi_ref, bbr_ref, bbi_ref, crt_ref, cit_ref):
    abr_ref[0], abi_ref[0], _, _ = _discretise(are_v[0], aim_v[0], ldt_v[0])
    _, _, f_re, f_im = _discretise(are_c[0], aim_c[0], ldt_c[0])
    br, bi = br_ref[0], bi_ref[0]
    gc, n = br.shape
    gn = abr_ref.shape[-1]
    spread = (lax.broadcasted_iota(jnp.int32, (n, gn), 1) % n
              == lax.broadcasted_iota(jnp.int32, (n, gn), 0)).astype(BF16)
    own = (lax.broadcasted_iota(jnp.int32, (gc, gn), 1) // n
           == lax.broadcasted_iota(jnp.int32, (gc, gn), 0) // (gc * n // gn))

    def block_diag(compact):
        return jnp.where(own, _dot(compact.astype(BF16), spread), 0.0).astype(BF16)

    bbr_ref[0] = block_diag(f_re * br - f_im * bi)
    bbi_ref[0] = block_diag(f_re * bi + f_im * br)
    crt_ref[0] = block_diag(cr_ref[0])
    cit_ref[0] = block_diag(ci_ref[0])


def _s5_prep(a_re, a_im, log_dt, b_re, b_im, c_re, c_im):
    nl, g, n, c = b_re.shape
    gn, gc = g * n, g * c
    row = lambda a: a.reshape(nl, 1, gn)
    per_channel = lambda a: jnp.repeat(a, c, axis=1)
    ldt = jnp.broadcast_to(log_dt[..., None], (nl, g, n))
    b_t = lambda a: jnp.swapaxes(a, 2, 3).reshape(nl, gc, n)
    vec = pl.BlockSpec((1, 1, gn), lambda l: (l, 0, 0))
    compact = pl.BlockSpec((1, gc, n), lambda l: (l, 0, 0))
    mat = pl.BlockSpec((1, gc, gn), lambda l: (l, 0, 0))
    return pl.pallas_call(
        _s5_prep_kernel,
        grid=(nl,),
        in_specs=[vec] * 3 + [compact] * 7,
        out_specs=[vec, vec] + [mat] * 4,
        out_shape=[jax.ShapeDtypeStruct((nl, 1, gn), F32)] * 2 + [jax.ShapeDtypeStruct((nl, gc, gn), BF16)] * 4,
        compiler_params=_params(1),
        name="s5_prep",
    )(row(a_re), row(a_im), row(ldt), per_channel(a_re), per_channel(a_im), per_channel(ldt),
      b_t(b_re), b_t(b_im), c_re.reshape(nl, gc, n), c_im.reshape(nl, gc, n))


def _s5_kernel(u_ref, h0r_ref, h0i_ref, abr_ref, abi_ref, bbr_ref, bbi_ref, crt_ref, cit_ref, d_ref,
               gw_ref, gb_ref, y_ref, hr_ref, hi_ref, ut, sr, si):
    n_chunk, nb, tc, _ = u_ref.shape
    gn = hr_ref.shape[-1]

    @pl.when(pl.program_id(0) == 0)
    def _():
        hr_ref[...] = h0r_ref[0]
        hi_ref[...] = h0i_ref[0]

    for j in range(n_chunk):
        for b in range(nb):
            ut[j, pl.ds(b, tc, stride=nb), :] = u_ref[j, b]
    u = jnp.concatenate([ut[j] for j in range(n_chunk)], axis=-1)
    ub = u.astype(BF16)
    sw = n_chunk * LANES
    dc, dn = sw // S5_DIAG, gn // S5_DIAG
    for i in range(S5_DIAG):
        ch, st = slice(i * dc, (i + 1) * dc), slice(i * dn, (i + 1) * dn)
        sr[:, st] = _dot(ub[:, ch], bbr_ref[0, ch, st])
        si[:, st] = _dot(ub[:, ch], bbi_ref[0, ch, st])

    cw = min(gn, SCAN_VREGS * SUBLANES * LANES // nb)
    for c0 in range(0, gn, cw):
        cols = slice(c0, c0 + cw)
        ar = jnp.broadcast_to(abr_ref[0, :, cols], (nb, cw))
        ai = jnp.broadcast_to(abi_ref[0, :, cols], (nb, cw))

        def step(t, h, cols=cols, ar=ar, ai=ai):
            hr, hi = h
            rows = pl.ds(pl.multiple_of(t * nb, nb), nb)
            nr = ar * hr - ai * hi + sr[rows, cols]
            ni = ar * hi + ai * hr + si[rows, cols]
            sr[rows, cols] = nr
            si[rows, cols] = ni
            return nr, ni

        hr, hi = lax.fori_loop(0, tc, step, (hr_ref[:, cols], hi_ref[:, cols]))
        hr_ref[:, cols] = hr
        hi_ref[:, cols] = hi

    ch_parts = []
    for i in range(S5_DIAG):
        ch, st = slice(i * dc, (i + 1) * dc), slice(i * dn, (i + 1) * dn)
        ch_parts.append(_nt(sr[:, st].astype(BF16), crt_ref[0, ch, st]) - _nt(si[:, st].astype(BF16), cit_ref[0, ch, st]))
    y = jnp.concatenate(ch_parts, axis=-1) + d_ref[0] * u
    z = jax.nn.gelu(y)
    out = z * jax.nn.sigmoid(_dot(z.astype(BF16), gw_ref[0]) + gb_ref[0])
    for j in range(n_chunk):
        ut[j] = out[:, j * LANES:(j + 1) * LANES]
    for j in range(n_chunk):
        for b in range(nb):
            y_ref[b, :, j * LANES:(j + 1) * LANES] = ut[j, pl.ds(b, tc, stride=nb), :]


def _s5(u, h0r, h0i, abr, abi, bbr, bbi, cr, ci, dskip, gw, gb, layer, h0_layer, nb, tc):
    n_chunk, m, _ = u.shape
    sw = n_chunk * LANES
    t = m // nb
    gn = h0r.shape[-1]
    assert nb % SUBLANES == 0 and tc % SUBLANES == 0
    assert (sw // SSM_GROUP_CH) % S5_DIAG == 0 and (sw // S5_DIAG) % LANES == 0 and (gn // S5_DIAG) % LANES == 0
    lay = lambda *shape: _resident((1,) + shape, lambda i: (layer,) + (0,) * len(shape))
    h0 = _resident((1, nb, gn), lambda i: (h0_layer, 0, 0))
    state = pl.BlockSpec((nb, gn), lambda i: (0, 0))
    y, h_re, h_im = pl.pallas_call(
        _s5_kernel,
        grid=(t // tc,),
        in_specs=[pl.BlockSpec((n_chunk, nb, tc, LANES), lambda i: (0, 0, i, 0)), h0, h0,
                  lay(1, gn), lay(1, gn), lay(sw, gn), lay(sw, gn), lay(sw, gn), lay(sw, gn),
                  lay(1, sw), lay(sw, sw), lay(1, sw)],
        out_specs=[pl.BlockSpec((nb, tc, sw), lambda i: (0, i, 0)), state, state],
        out_shape=[jax.ShapeDtypeStruct((nb, t, sw), F32)] + [jax.ShapeDtypeStruct((nb, gn), F32)] * 2,
        scratch_shapes=[pltpu.VMEM((n_chunk, tc * nb, LANES), F32)] + [pltpu.VMEM((tc * nb, gn), F32)] * 2,
        compiler_params=_params(1),
        name="s5",
    )(u.reshape(n_chunk, nb, t, LANES), h0r, h0i, abr, abi, bbr, bbi, cr, ci, dskip, gw, gb)
    return y.reshape(m, sw), h_re, h_im


def _mix_ffn_kernel(x_ref, a_ref, s_ref, ag_ref, sg_ref, wo_ref, n2_ref, wg_ref, wu_ref, wd_ref, fg_ref,
                    y_ref, *, ff_split, final):
    aw = a_ref.shape[-1]
    an = _rms(a_ref[...], ag_ref[0]).astype(BF16)
    sn = _rms(s_ref[...], sg_ref[0]).astype(BF16)
    h = x_ref[...] + (_dot(an, wo_ref[0, :aw, :]) + _dot(sn, wo_ref[0, aw:, :]))
    hn = _rms(h, n2_ref[0]).astype(BF16)
    ffn = None
    for cols in (slice(0, ff_split), slice(ff_split, wg_ref.shape[-1])):
        act = jax.nn.silu(_dot(hn, wg_ref[0, :, cols])) * _dot(hn, wu_ref[0, :, cols])
        part = _dot(act.astype(BF16), wd_ref[0, cols, :])
        ffn = part if ffn is None else ffn + part
    y = h + ffn
    if final:
        y = _rms(y, fg_ref[...])
    y_ref[...] = y


def _mix_ffn(x, attn, ssm, ag, sg, wo, n2, wg, wu, wd, fg, layer, final):
    m, dm = x.shape
    aw, sw, dff = attn.shape[-1], ssm.shape[-1], wg.shape[-1]
    ff_split = pl.cdiv(pl.cdiv(dff, MXU_TILE), 2) * MXU_TILE
    assert 0 < ff_split < dff and dff % LANES == 0
    tm = min(m, FFN_ROW_TILE)
    row = lambda width: pl.BlockSpec((tm, width), lambda i: (i, 0))
    lay = lambda *shape: _resident((1,) + shape, lambda i: (layer,) + (0,) * len(shape))
    return pl.pallas_call(
        functools.partial(_mix_ffn_kernel, ff_split=ff_split, final=final),
        grid=(m // tm,),
        in_specs=[row(dm), row(aw), row(sw), lay(1, aw), lay(1, sw), lay(aw + sw, dm), lay(1, dm),
                  lay(dm, dff), lay(dm, dff), lay(dff, dm), _resident((1, dm), lambda i: (0, 0))],
        out_specs=row(dm),
        out_shape=jax.ShapeDtypeStruct((m, dm), F32),
        compiler_params=_params(1),
        name="mix_ffn",
    )(x, attn, ssm, ag, sg, wo, n2, wg, wu, wd, fg)


def kernel(x_prompt, x_sample, cache_attn_k, cache_attn_v, state_ssm_re, state_ssm_im, norm1_g, w_in, attn_out_g, ssm_a_re, ssm_a_im, ssm_log_dt, ssm_b_re, ssm_b_im, ssm_c_re, ssm_c_im, ssm_d, ssm_glu_w, ssm_glu_b, ssm_out_g, w_out, norm2_g, ffn_w_gate, ffn_w_up, ffn_w_down, final_norm_g):
    bp, seq, dm = x_prompt.shape
    bs, tdec, _ = x_sample.shape
    depth, _, lbuf, n_heads, head_dim = cache_attn_k.shape
    _, n_groups, state_n = ssm_a_re.shape
    gn = n_groups * state_n
    aw = n_heads * head_dim
    assert head_dim == HEAD_DIM and state_n == STATE_N and lbuf == min(max(w for w, _ in BRANCHES), lbuf)

    vec = lambda a: a.reshape(depth, 1, -1)
    w_in_b, w_out_b = w_in.astype(BF16), w_out.astype(BF16)
    wg_b, wu_b, wd_b = ffn_w_gate.astype(BF16), ffn_w_up.astype(BF16), ffn_w_down.astype(BF16)
    glu_w_b = ssm_glu_w.astype(BF16)
    s5_mats = _s5_prep(ssm_a_re, ssm_a_im, ssm_log_dt, ssm_b_re, ssm_b_im, ssm_c_re, ssm_c_im)
    n1, n2, ag, sg = vec(norm1_g), vec(norm2_g), vec(attn_out_g), vec(ssm_out_g)
    dsk, gb = vec(ssm_d), vec(ssm_glu_b)
    fg = final_norm_g.reshape(1, dm)
    kc = jnp.transpose(cache_attn_k, (0, 1, 3, 4, 2))
    vc = jnp.transpose(cache_attn_v, (0, 1, 3, 4, 2))
    h0p = jnp.zeros((1, bp, gn), F32)
    h0s_re = state_ssm_re.reshape(depth, bs, gn)
    h0s_im = state_ssm_im.reshape(depth, bs, gn)

    xp = x_prompt.reshape(bp * seq, dm)
    xs = x_sample.reshape(bs * tdec, dm)
    kt = vt = None
    pr, pim, sk, sv, sr, sim = ([] for _ in range(6))
    for l in range(depth):
        final = l == depth - 1
        s5w = (*s5_mats, dsk, glu_w_b, gb)
        ffw = (ag, sg, w_out_b, n2, wg_b, wu_b, wd_b, fg)

        q, k, v, u, kt, vt = _in_proj(xp, n1, w_in_b, l, kv_buffers=((depth, bp, aw, seq), kt, vt), seq=seq)
        attn = _attn_prompt(q, k, v, seq)
        ssm, h_re, h_im = _s5(u, h0p, h0p, *s5w, layer=l, h0_layer=0, nb=bp, tc=SCAN_CHUNK)
        xp = _mix_ffn(xp, attn, ssm, *ffw, layer=l, final=final)
        pr.append(h_re.reshape(bp, n_groups, state_n))
        pim.append(h_im.reshape(bp, n_groups, state_n))

        q, k, v, u = _in_proj(xs, n1, w_in_b, l)
        attn = _attn_sample(q, k, v, kc, vc, l, tdec)
        ssm, h_re, h_im = _s5(u, h0s_re, h0s_im, *s5w, layer=l, h0_layer=l, nb=bs, tc=tdec)
        xs = _mix_ffn(xs, attn, ssm, *ffw, layer=l, final=final)
        sk.append(k.reshape(bs, tdec, n_heads, head_dim))
        sv.append(v.reshape(bs, tdec, n_heads, head_dim))
        sr.append(h_re.reshape(bs, n_groups, state_n))
        sim.append(h_im.reshape(bs, n_groups, state_n))

    window = lambda t: jnp.transpose(t.reshape(depth, bp, n_heads, head_dim, seq), (0, 1, 4, 2, 3))[:, :, -lbuf:]
    return (xp.reshape(bp, seq, dm), xs.reshape(bs, tdec, dm),
            window(kt), window(vt), jnp.stack(pr), jnp.stack(pim),
            jnp.stack(sk), jnp.stack(sv), jnp.stack(sr), jnp.stack(sim))
```
